```python
import math
import jax, jax.numpy as jnp
from jax import lax
import numpy as np

D_MODEL = 1024
BATCH = 1
SEQ = 16384
DEPTH = 1

D_MIX = D_MODEL
D_HGRN = D_MIX // 2
D_DIFF = D_MIX - D_HGRN
HGRN_EXPAND = 128
HGRN_HEADS = D_HGRN // HGRN_EXPAND
HGRN_DK = HGRN_EXPAND
HGRN_DV = D_HGRN // HGRN_HEADS
HGRN_CHUNK = 64
DIFF_HEADS = 4
DIFF_VDIM = D_DIFF // DIFF_HEADS
DIFF_QKDIM = DIFF_VDIM // 2
Q_BLOCK = 128
ROPE_THETA = 10000.0
NORM_EPS = 1e-6
SUBLN_EPS = 1e-5
LAMBDA_STD = 0.1
SPLIT_WIDTHS = (D_HGRN, D_HGRN, D_HGRN, D_HGRN,
                D_DIFF, D_DIFF, D_DIFF, D_DIFF)
D_IN = sum(SPLIT_WIDTHS)
SPLIT_POINTS = tuple(int(v) for v in np.cumsum(SPLIT_WIDTHS)[:-1])

kernel_name = "hymba_style_hgrn2_diffattn_hybrid"


def rmsnorm(x, w, eps=NORM_EPS):
    xf = x.astype(jnp.float32)
    y = xf * lax.rsqrt(jnp.mean(xf * xf, axis=-1, keepdims=True) + eps)
    return (y * w.astype(jnp.float32)).astype(x.dtype)


def rope_tables(T):
    half = DIFF_QKDIM // 2
    inv_freq = 1.0 / (ROPE_THETA ** (jnp.arange(half, dtype=jnp.float32) / half))
    ang = jnp.arange(T, dtype=jnp.float32)[:, None] * inv_freq[None, :]
    return jnp.cos(ang), jnp.sin(ang)


def apply_rope(t, cos, sin):
    c = cos[:, None, None, :].astype(t.dtype)
    s = sin[:, None, None, :].astype(t.dtype)
    t1, t2 = jnp.split(t, 2, axis=-1)
    return jnp.concatenate([t1 * c - t2 * s, t2 * c + t1 * s], axis=-1)


def hgrn2_mixer(q, f_pre, i, lb):
    B, T, _ = q.shape
    N = T // HGRN_CHUNK
    f = lb + (1.0 - lb) * jax.nn.sigmoid(f_pre.astype(jnp.float32))
    k = 1.0 - f
    g = jnp.log(f)

    def heads(t, d):
        return t.astype(jnp.float32).reshape(B, N, HGRN_CHUNK, HGRN_HEADS, d).transpose(1, 0, 3, 2, 4)

    qc, kc, gc = heads(q, HGRN_DK), heads(k, HGRN_DK), heads(g, HGRN_DK)
    vc = heads(i, HGRN_DV)
    bc = jnp.cumsum(gc, axis=3)
    causal = jnp.tril(jnp.ones((HGRN_CHUNK, HGRN_CHUNK), dtype=bool))[None, None, :, :, None]

    def step(S, inp):
        q_, k_, v_, b_ = inp
        inter = jnp.einsum('bhcd,bhde->bhce', q_ * jnp.exp(b_), S)
        diff = b_[:, :, :, None, :] - b_[:, :, None, :, :]
        decay = jnp.where(causal, jnp.exp(jnp.where(causal, diff, 0.0)), 0.0)
        A = jnp.einsum('bhtd,bhsd,bhtsd->bhts', q_, k_, decay)
        intra = jnp.einsum('bhts,bhse->bhte', A, v_)
        b_last = b_[:, :, -1, :]
        S_new = jnp.exp(b_last)[..., None] * S + jnp.einsum(
            'bhsd,bhse->bhde', k_ * jnp.exp(b_last[:, :, None, :] - b_), v_)
        return S_new, inter + intra

    S0 = jnp.zeros((B, HGRN_HEADS, HGRN_DK, HGRN_DV), jnp.float32)
    _, ys = lax.scan(step, S0, (qc, kc, vc, bc))
    return ys.transpose(1, 0, 3, 2, 4).reshape(B, T, HGRN_HEADS, HGRN_DV)


def diff_attention(q, k, v, lam):
    B, H, _, T, dh = q.shape
    NB = T // Q_BLOCK
    scale = dh ** -0.5
    q_blocks = q.reshape(B, H, 2, NB, Q_BLOCK, dh).transpose(3, 0, 1, 2, 4, 5)
    q_pos = jnp.arange(T).reshape(NB, Q_BLOCK)
    k_pos = jnp.arange(T)

    def attend(args):
        qb, qp = args
        s = jnp.einsum('bhcqd,bhckd->bhcqk', qb, k).astype(jnp.float32) * scale
        s = jnp.where((qp[:, None] >= k_pos[None, :])[None, None, None], s, -jnp.inf)
        p = jax.nn.softmax(s, axis=-1)
        w = p[:, :, 0] - lam * p[:, :, 1]
        return jnp.einsum('bhqk,bhke->bhqe', w.astype(v.dtype), v)

    o = lax.map(attend, (q_blocks, q_pos))
    return o.transpose(1, 0, 3, 2, 4).reshape(B, T, H, DIFF_VDIM)


def setup_inputs(seed: int = 0) -> dict:
    key = jax.random.key(seed)
    ks = jax.random.split(key, 12)
    f32 = jnp.float32
    x = jax.random.normal(ks[0], (BATCH, SEQ, D_MODEL), f32)
    norm_w = 1.0 + 0.02 * jax.random.normal(ks[1], (DEPTH, D_MODEL), f32)
    w_in = jax.random.normal(ks[2], (DEPTH, D_MODEL, D_IN), f32) * D_MODEL ** -0.5
    hgrn_lb_logits = 0.1 * jax.random.normal(ks[3], (DEPTH + 1, D_HGRN), f32)
    hgrn_norm_w = 1.0 + 0.02 * jax.random.normal(ks[4], (DEPTH, HGRN_DV), f32)
    diff_lambda_q1 = LAMBDA_STD * jax.random.normal(ks[5], (DEPTH, DIFF_QKDIM), f32)
    diff_lambda_k1 = LAMBDA_STD * jax.random.normal(ks[6], (DEPTH, DIFF_QKDIM), f32)
    diff_lambda_q2 = LAMBDA_STD * jax.random.normal(ks[7], (DEPTH, DIFF_QKDIM), f32)
    diff_lambda_k2 = LAMBDA_STD * jax.random.normal(ks[8], (DEPTH, DIFF_QKDIM), f32)
    diff_norm_w = 1.0 + 0.02 * jax.random.normal(ks[9], (DEPTH, DIFF_VDIM), f32)
    w_out = jax.random.normal(ks[10], (DEPTH, D_MIX, D_MODEL), f32) * (D_MIX * 2 * DEPTH) ** -0.5
    final_norm_w = 1.0 + 0.02 * jax.random.normal(ks[11], (D_MODEL,), f32)
    return {"x": x, "norm_w": norm_w, "w_in": w_in, "hgrn_lb_logits": hgrn_lb_logits,
            "hgrn_norm_w": hgrn_norm_w, "diff_lambda_q1": diff_lambda_q1,
            "diff_lambda_k1": diff_lambda_k1, "diff_lambda_q2": diff_lambda_q2,
            "diff_lambda_k2": diff_lambda_k2, "diff_norm_w": diff_norm_w,
            "w_out": w_out, "final_norm_w": final_norm_w}


def reference(x, norm_w, w_in, hgrn_lb_logits, hgrn_norm_w, diff_lambda_q1, diff_lambda_k1,
              diff_lambda_q2, diff_lambda_k2, diff_norm_w, w_out, final_norm_w):
    B, T, _ = x.shape
    cos, sin = rope_tables(T)
    lb_all = jnp.cumsum(jax.nn.softmax(hgrn_lb_logits.astype(jnp.float32), axis=0), axis=0)
    for l in range(DEPTH):
        h = rmsnorm(x, norm_w[l])
        proj = h @ w_in[l].astype(h.dtype)
        hq, hf, hi, hg, dq, dk, dv, dg = jnp.split(proj, SPLIT_POINTS, axis=-1)

        ho = hgrn2_mixer(hq, hf, hi, lb_all[l])
        ho = rmsnorm(ho, hgrn_norm_w[l]).reshape(B, T, D_HGRN)
        ho = ho.astype(x.dtype) * jax.nn.silu(hg)

        qd = apply_rope(dq.reshape(B, T, DIFF_HEADS, 2, DIFF_QKDIM), cos, sin).transpose(0, 2, 3, 1, 4)
        kd = apply_rope(dk.reshape(B, T, DIFF_HEADS, 2, DIFF_QKDIM), cos, sin).transpose(0, 2, 3, 1, 4)
        vd = dv.reshape(B, T, DIFF_HEADS, DIFF_VDIM).transpose(0, 2, 1, 3)
        lambda_init = 0.8 - 0.6 * math.exp(-0.3 * l)
        lam = (jnp.exp(jnp.sum(diff_lambda_q1[l].astype(jnp.float32) * diff_lambda_k1[l].astype(jnp.float32)))
               - jnp.exp(jnp.sum(diff_lambda_q2[l].astype(jnp.float32) * diff_lambda_k2[l].astype(jnp.float32)))
               + lambda_init)
        do = diff_attention(qd, kd, vd, lam)
        do = rmsnorm(do, diff_norm_w[l], SUBLN_EPS) * (1.0 - lambda_init)
        do = do.reshape(B, T, D_DIFF).astype(x.dtype) * jax.nn.silu(dg)

        mix = jnp.concatenate([ho, do], axis=-1)
        x = x + mix @ w_out[l].astype(mix.dtype)
    return rmsnorm(x, final_norm_w)
```

```python
import functools
import math

import jax
import jax.numpy as jnp
from jax import lax
from jax.experimental import pallas as pl
from jax.experimental.pallas import tpu as pltpu

F32 = jnp.float32
BF16 = jnp.bfloat16

LANES = 128
VMEM_LIMIT = 56 * 1024 * 1024

N_HEADS = 4
HEAD_W = 128
GROUP_W = N_HEADS * HEAD_W
QK_DIM = 64
ROPE_THETA = 10000.0
NORM_EPS = 1e-6
SUBLN_EPS = 1e-5

PROJ_TM = 512
HGRN_CHUNK = 64
HGRN_SUB = 16
HGRN_TM = 512
ATT_TQ = 512
ATT_TK = 512

NT = (((1,), (1,)), ((), ()))
TN = (((0,), (0,)), ((), ()))


def _silu(x):
    return x * (1.0 / (1.0 + jnp.exp(-x)))


def _in_proj_kernel(x_ref, nw_ref, w_ref, lbl_ref, cos_ref, sin_ref,
                    hq_ref, hk_ref, hgl_ref, hi_ref, hgate_ref,
                    dq_ref, dk_ref, dvt_ref, dgate_ref):
    x = x_ref[...]
    ms = jnp.mean(x * x, axis=-1, keepdims=True)
    h = (x * lax.rsqrt(ms + NORM_EPS) * nw_ref[...]).astype(BF16)

    def seg(i):
        return jnp.dot(h, w_ref[:, i * GROUP_W:(i + 1) * GROUP_W],
                       preferred_element_type=F32)

    lg = lbl_ref[...]
    e = jnp.exp(lg - jnp.max(lg, axis=0, keepdims=True))
    lb = e[0:1, :] / jnp.sum(e, axis=0, keepdims=True)

    hq_ref[...] = seg(0)
    f = lb + (1.0 - lb) * (1.0 / (1.0 + jnp.exp(-seg(1))))
    hk_ref[...] = 1.0 - f
    hgl_ref[...] = jnp.log(f)
    hi_ref[...] = seg(2)
    hgate_ref[...] = _silu(seg(3))

    cos = cos_ref[...]
    sin = sin_ref[...]
    lane = lax.broadcasted_iota(jnp.int32, cos.shape, 1)
    first_half = (lane % QK_DIM) < (QK_DIM // 2)

    def rope(t, scale):
        outs = []
        for hh in range(N_HEADS):
            th = t[:, hh * HEAD_W:(hh + 1) * HEAD_W]
            up = pltpu.roll(th, HEAD_W - QK_DIM // 2, 1)
            dn = pltpu.roll(th, QK_DIM // 2, 1)
            partner = jnp.where(first_half, up, dn)
            outs.append(((th * cos + partner * sin) * scale).astype(BF16))
        return jnp.concatenate(outs, axis=1)

    dq_ref[...] = rope(seg(4), QK_DIM ** -0.5)
    dk_ref[...] = rope(seg(5), 1.0)
    dvt_ref[...] = seg(6).T.astype(BF16)
    dgate_ref[...] = _silu(seg(7))


def _in_proj(x2, norm_w, w_in_bf, lb_logits, cos_t, sin_t):
    T, D = x2.shape
    n = T // PROJ_TM
    row = lambda i: (i, 0)
    fix = lambda i: (0, 0)
    f32_out = jax.ShapeDtypeStruct((T, GROUP_W), F32)
    bf_out = jax.ShapeDtypeStruct((T, GROUP_W), BF16)
    blk = pl.BlockSpec((PROJ_TM, GROUP_W), row)
    return pl.pallas_call(
        _in_proj_kernel,
        grid=(n,),
        in_specs=[
            pl.BlockSpec((PROJ_TM, D), row),
            pl.BlockSpec((1, D), fix),
            pl.BlockSpec(w_in_bf.shape, fix),
            pl.BlockSpec(lb_logits.shape, fix),
            pl.BlockSpec((PROJ_TM, LANES), row),
            pl.BlockSpec((PROJ_TM, LANES), row),
        ],
        out_specs=[blk, blk, blk, blk, blk, blk, blk,
                   pl.BlockSpec((GROUP_W, PROJ_TM), lambda i: (0, i)), blk],
        out_shape=[f32_out, f32_out, f32_out, f32_out, f32_out, bf_out, bf_out,
                   jax.ShapeDtypeStruct((GROUP_W, T), BF16), f32_out],
        compiler_params=pltpu.CompilerParams(
            dimension_semantics=("arbitrary",), vmem_limit_bytes=VMEM_LIMIT),
        name="in_proj",
    )(x2, norm_w, w_in_bf, lb_logits, cos_t, sin_t)


def _hgrn_kernel(q_ref, k_ref, g_ref, v_ref, gate_ref, nw_ref, o_ref, st_ref):
    C, SB = HGRN_CHUNK, HGRN_SUB

    @pl.when(pl.program_id(0) == 0)
    def _():
        st_ref[...] = jnp.zeros_like(st_ref)

    ti = lax.broadcasted_iota(jnp.int32, (C, C), 0)
    si = lax.broadcasted_iota(jnp.int32, (C, C), 1)
    tri = (si <= ti).astype(F32)
    sub_t = lax.broadcasted_iota(jnp.int32, (SB, HEAD_W), 0)
    nw = nw_ref[...]

    def chunk(c, carry):
        r0 = pl.multiple_of(c * C, C)
        rows = pl.ds(r0, C)
        b_all = jnp.dot(tri, g_ref[rows, :], preferred_element_type=F32,
                        precision=lax.Precision.HIGHEST)
        for hh in range(N_HEADS):
            cols = slice(hh * HEAD_W, (hh + 1) * HEAD_W)
            b = b_all[:, cols]
            q = q_ref[rows, cols]
            k = k_ref[rows, cols]
            v = v_ref[rows, cols]
            v_bf = v.astype(BF16)
            st = st_ref[hh]
            inter = lax.dot_general((q * jnp.exp(b)).astype(BF16), st.astype(BF16),
                                    NT, preferred_element_type=F32)
            b_last = b[C - 1:C, :]
            k_dec = (k * jnp.exp(b_last - b)).astype(BF16)
            st_ref[hh] = st * jnp.exp(b_last) + lax.dot_general(
                v_bf, k_dec, TN, preferred_element_type=F32)

            outs = []
            for blk in range(C // SB):
                lo = blk * SB
                b_i = b[lo:lo + SB, :]
                q_i = q[lo:lo + SB, :]
                k_i = k[lo:lo + SB, :]
                v_i = v[lo:lo + SB, :]
                o_i = inter[lo:lo + SB, :]
                if blk > 0:
                    ref = b[lo - 1:lo, :]
                    q_t = (q_i * jnp.exp(b_i - ref)).astype(BF16)
                    k_t = (k[:lo, :] * jnp.exp(ref - b[:lo, :])).astype(BF16)
                    a = lax.dot_general(q_t, k_t, NT, preferred_element_type=F32)
                    o_i = o_i + jnp.dot(a.astype(BF16), v_bf[:lo, :],
                                        preferred_element_type=F32)
                qk_base = q_i
                for s in range(SB):
                    diff = jnp.where(sub_t >= s, b_i - b_i[s:s + 1, :], -jnp.inf)
                    w = qk_base * k_i[s:s + 1, :] * jnp.exp(diff)
                    o_i = o_i + jnp.sum(w, axis=-1, keepdims=True) * v_i[s:s + 1, :]
                outs.append(o_i)
            o = jnp.concatenate(outs, axis=0)
            ms = jnp.mean(o * o, axis=-1, keepdims=True)
            o_ref[rows, cols] = o * lax.rsqrt(ms + NORM_EPS) * nw * gate_ref[rows, cols]
        return carry

    lax.fori_loop(0, HGRN_TM // C, chunk, 0)


def _hgrn(hq, hk, hgl, hi, hgate, norm_w):
    T = hq.shape[0]
    blk = pl.BlockSpec((HGRN_TM, GROUP_W), lambda i: (i, 0))
    return pl.pallas_call(
        _hgrn_kernel,
        grid=(T // HGRN_TM,),
        in_specs=[blk, blk, blk, blk, blk, pl.BlockSpec((1, HEAD_W), lambda i: (0, 0))],
        out_specs=blk,
        out_shape=jax.ShapeDtypeStruct((T, GROUP_W), F32),
        scratch_shapes=[pltpu.VMEM((N_HEADS, HEAD_W, HEAD_W), F32)],
        compiler_params=pltpu.CompilerParams(
            dimension_semantics=("arbitrary",), vmem_limit_bytes=VMEM_LIMIT),
        name="hgrn2",
    )(hq, hk, hgl, hi, hgate, norm_w)


def _attn_kernel(lq1_ref, lk1_ref, lq2_ref, lk2_ref, q_ref, k_ref, vt_ref, gate_ref,
                 nw_ref, o_ref, m_ref, l_ref, acc_ref, *, lambda_init):
    qi = pl.program_id(1)
    lam = (jnp.exp(jnp.sum(lq1_ref[...] * lk1_ref[...]))
           - jnp.exp(jnp.sum(lq2_ref[...] * lk2_ref[...])) + lambda_init)

    q = q_ref[...]
    lane = lax.broadcasted_iota(jnp.int32, q.shape, 1)
    zero = jnp.zeros_like(q)
    q_maps = (jnp.where(lane < QK_DIM, q, zero), jnp.where(lane >= QK_DIM, q, zero))

    m_ref[...] = jnp.full_like(m_ref, -jnp.inf)
    l_ref[...] = jnp.zeros_like(l_ref)
    acc_ref[...] = jnp.zeros_like(acc_ref)

    def block(j, masked):
        k0 = pl.multiple_of(j * ATT_TK, ATT_TK)
        kb = k_ref[pl.ds(k0, ATT_TK), :]
        vt = vt_ref[:, pl.ds(k0, ATT_TK)]
        for c in range(2):
            s = lax.dot_general(kb, q_maps[c], NT, preferred_element_type=F32)
            if masked:
                kpos = lax.broadcasted_iota(jnp.int32, s.shape, 0)
                qpos = lax.broadcasted_iota(jnp.int32, s.shape, 1)
                s = jnp.where(kpos <= qpos, s, -jnp.inf)
            m_old = m_ref[c]
            m_new = jnp.maximum(m_old, jnp.max(s, axis=0, keepdims=True))
            alpha = jnp.exp(m_old - m_new)
            p = jnp.exp(s - m_new)
            l_ref[c] = alpha * l_ref[c] + jnp.sum(p, axis=0, keepdims=True)
            acc_ref[c] = alpha * acc_ref[c] + jnp.dot(
                vt, p.astype(BF16), preferred_element_type=F32)
            m_ref[c] = m_new

    def body(j, carry):
        block(j, False)
        return carry

    lax.fori_loop(0, qi, body, 0)
    block(qi, True)

    o_t = acc_ref[0] / l_ref[0] - lam * (acc_ref[1] / l_ref[1])
    o = o_t.T
    ms = jnp.mean(o * o, axis=-1, keepdims=True)
    o = o * lax.rsqrt(ms + SUBLN_EPS) * nw_ref[...] * (1.0 - lambda_init)
    o_ref[...] = o * gate_ref[...]


def _diff_attn(lq1, lk1, lq2, lk2, dq, dk, dvt, dgate, norm_w, lambda_init):
    T = dq.shape[0]
    lam_spec = pl.BlockSpec((1, QK_DIM), lambda h, i: (0, 0))
    return pl.pallas_call(
        functools.partial(_attn_kernel, lambda_init=lambda_init),
        grid=(N_HEADS, T // ATT_TQ),
        in_specs=[
            lam_spec, lam_spec, lam_spec, lam_spec,
            pl.BlockSpec((ATT_TQ, HEAD_W), lambda h, i: (i, h)),
            pl.BlockSpec((T, HEAD_W), lambda h, i: (0, h)),
            pl.BlockSpec((HEAD_W, T), lambda h, i: (h, 0)),
            pl.BlockSpec((ATT_TQ, HEAD_W), lambda h, i: (i, h)),
            pl.BlockSpec((1, HEAD_W), lambda h, i: (0, 0)),
        ],
        out_specs=pl.BlockSpec((ATT_TQ, HEAD_W), lambda h, i: (i, h)),
        out_shape=jax.ShapeDtypeStruct((T, GROUP_W), F32),
        scratch_shapes=[
            pltpu.VMEM((2, 1, ATT_TQ), F32),
            pltpu.VMEM((2, 1, ATT_TQ), F32),
            pltpu.VMEM((2, HEAD_W, ATT_TQ), F32),
        ],
        compiler_params=pltpu.CompilerParams(
            dimension_semantics=("arbitrary", "arbitrary"), vmem_limit_bytes=VMEM_LIMIT),
        name="diff_attn",
    )(lq1, lk1, lq2, lk2, dq, dk, dvt, dgate, norm_w)


def _out_proj_kernel(ho_ref, do_ref, x_ref, w_ref, nw_ref, o_ref):
    y = x_ref[...]
    y = y + jnp.dot(ho_ref[...].astype(BF16), w_ref[:GROUP_W, :], preferred_element_type=F32)
    y = y + jnp.dot(do_ref[...].astype(BF16), w_ref[GROUP_W:, :], preferred_element_type=F32)
    ms = jnp.mean(y * y, axis=-1, keepdims=True)
    o_ref[...] = y * lax.rsqrt(ms + NORM_EPS) * nw_ref[...]


def _out_proj(ho, do, x2, w_out_bf, final_norm_w):
    T, D = x2.shape
    row = lambda i: (i, 0)
    fix = lambda i: (0, 0)
    return pl.pallas_call(
        _out_proj_kernel,
        grid=(T // PROJ_TM,),
        in_specs=[
            pl.BlockSpec((PROJ_TM, GROUP_W), row),
            pl.BlockSpec((PROJ_TM, GROUP_W), row),
            pl.BlockSpec((PROJ_TM, D), row),
            pl.BlockSpec(w_out_bf.shape, fix),
            pl.BlockSpec((1, D), fix),
        ],
        out_specs=pl.BlockSpec((PROJ_TM, D), row),
        out_shape=jax.ShapeDtypeStruct((T, D), F32),
        compiler_params=pltpu.CompilerParams(
            dimension_semantics=("arbitrary",), vmem_limit_bytes=VMEM_LIMIT),
        name="out_proj",
    )(ho, do, x2, w_out_bf, final_norm_w)


def _rope_tables(T):
    half = QK_DIM // 2
    inv_freq = 1.0 / (ROPE_THETA ** (jnp.arange(half, dtype=F32) / half))
    ang = jnp.arange(T, dtype=F32)[:, None] * inv_freq[None, :]
    cos, sin = jnp.cos(ang), jnp.sin(ang)
    reps = LANES // QK_DIM
    cos_t = jnp.tile(jnp.concatenate([cos, cos], axis=1), (1, reps))
    sin_t = jnp.tile(jnp.concatenate([-sin, sin], axis=1), (1, reps))
    return cos_t, sin_t


def kernel(x, norm_w, w_in, hgrn_lb_logits, hgrn_norm_w, diff_lambda_q1, diff_lambda_k1,
           diff_lambda_q2, diff_lambda_k2, diff_norm_w, w_out, final_norm_w):
    B, T, D = x.shape
    depth = norm_w.shape[0]
    assert B == 1 and depth == 1 and D == 2 * GROUP_W
    assert T % ATT_TQ == 0 and T % HGRN_TM == 0 and T % PROJ_TM == 0
    x2 = x.reshape(T, D)
    cos_t, sin_t = _rope_tables(T)
    l = 0
    lambda_init = 0.8 - 0.6 * math.exp(-0.3 * l)
    hq, hk, hgl, hi, hgate, dq, dk, dvt, dgate = _in_proj(
        x2, norm_w[l:l + 1], w_in[l].astype(BF16), hgrn_lb_logits, cos_t, sin_t)
    ho = _hgrn(hq, hk, hgl, hi, hgate, hgrn_norm_w[l:l + 1])
    do = _diff_attn(diff_lambda_q1[l:l + 1], diff_lambda_k1[l:l + 1],
                    diff_lambda_q2[l:l + 1], diff_lambda_k2[l:l + 1],
                    dq, dk, dvt, dgate, diff_norm_w[l:l + 1], lambda_init)
    out = _out_proj(ho, do, x2, w_out[l].astype(BF16), final_norm_w.reshape(1, D))
    return out.reshape(B, T, D)
```

```python
import functools
import math

import jax
import jax.numpy as jnp
from jax import lax
from jax.experimental import pallas as pl
from jax.experimental.pallas import tpu as pltpu

F32 = jnp.float32
BF16 = jnp.bfloat16

LANES = 128
VMEM_LIMIT = 56 * 1024 * 1024

N_HEADS = 4
HEAD_W = 128
GROUP_W = N_HEADS * HEAD_W
QK_DIM = 64
ROPE_THETA = 10000.0
NORM_EPS = 1e-6
SUBLN_EPS = 1e-5

PROJ_TM = 512
HGRN_CHUNK = 64
HGRN_SUB = 16
HGRN_TM = 512
ATT_TQ = 512
ATT_TK = 512
ATT_TN = 256
VT_ROWS = HEAD_W + 16
LOG2E = math.log2(math.e)

NT = (((1,), (1,)), ((), ()))
TN = (((0,), (0,)), ((), ()))


def _silu(x):
    return x * (1.0 / (1.0 + jnp.exp(-x)))


def _in_proj_kernel(x_ref, nw_ref, w_ref, lbl_ref, cos_ref, sin_ref,
                    hq_ref, hk_ref, hgl_ref, hi_ref, hgate_ref,
                    dq_ref, dk_ref, dvt_ref, dgate_ref):
    x = x_ref[...]
    ms = jnp.mean(x * x, axis=-1, keepdims=True)
    h = (x * lax.rsqrt(ms + NORM_EPS) * nw_ref[...]).astype(BF16)

    def seg(i):
        return jnp.dot(h, w_ref[:, i * GROUP_W:(i + 1) * GROUP_W],
                       preferred_element_type=F32)

    lg = lbl_ref[...]
    e = jnp.exp(lg - jnp.max(lg, axis=0, keepdims=True))
    lb = e[0:1, :] / jnp.sum(e, axis=0, keepdims=True)

    hq_ref[...] = seg(0)
    f = lb + (1.0 - lb) * (1.0 / (1.0 + jnp.exp(-seg(1))))
    hk_ref[...] = 1.0 - f
    hgl_ref[...] = jnp.log(f)
    hi_ref[...] = seg(2)
    hgate_ref[...] = _silu(seg(3))

    cos = cos_ref[...]
    sin = sin_ref[...]
    lane = lax.broadcasted_iota(jnp.int32, cos.shape, 1)
    first_half = (lane % QK_DIM) < (QK_DIM // 2)

    def rope(t, scale):
        outs = []
        for hh in range(N_HEADS):
            th = t[:, hh * HEAD_W:(hh + 1) * HEAD_W]
            up = pltpu.roll(th, HEAD_W - QK_DIM // 2, 1)
            dn = pltpu.roll(th, QK_DIM // 2, 1)
            partner = jnp.where(first_half, up, dn)
            outs.append(((th * cos + partner * sin) * scale).astype(BF16))
        return jnp.concatenate(outs, axis=1)

    dq_ref[...] = rope(seg(4), QK_DIM ** -0.5 * LOG2E)
    dk_ref[...] = rope(seg(5), 1.0)
    vt = seg(6).T.astype(BF16)
    for hh in range(N_HEADS):
        dvt_ref[hh, :HEAD_W, :] = vt[hh * HEAD_W:(hh + 1) * HEAD_W, :]
        dvt_ref[hh, HEAD_W:, :] = jnp.ones((VT_ROWS - HEAD_W, PROJ_TM), BF16)
    dgate_ref[...] = _silu(seg(7))


def _in_proj(x2, norm_w, w_in_bf, lb_logits, cos_t, sin_t):
    T, D = x2.shape
    n = T // PROJ_TM
    row = lambda i: (i, 0)
    fix = lambda i: (0, 0)
    f32_out = jax.ShapeDtypeStruct((T, GROUP_W), F32)
    bf_out = jax.ShapeDtypeStruct((T, GROUP_W), BF16)
    blk = pl.BlockSpec((PROJ_TM, GROUP_W), row)
    return pl.pallas_call(
        _in_proj_kernel,
        grid=(n,),
        in_specs=[
            pl.BlockSpec((PROJ_TM, D), row),
            pl.BlockSpec((1, D), fix),
            pl.BlockSpec(w_in_bf.shape, fix),
            pl.BlockSpec(lb_logits.shape, fix),
            pl.BlockSpec((PROJ_TM, LANES), row),
            pl.BlockSpec((PROJ_TM, LANES), row),
        ],
        out_specs=[blk, blk, blk, blk, blk, blk, blk,
                   pl.BlockSpec((N_HEADS, VT_ROWS, PROJ_TM), lambda i: (0, 0, i)), blk],
        out_shape=[f32_out, f32_out, f32_out, f32_out, f32_out, bf_out, bf_out,
                   jax.ShapeDtypeStruct((N_HEADS, VT_ROWS, T), BF16), f32_out],
        compiler_params=pltpu.CompilerParams(
            dimension_semantics=("arbitrary",), vmem_limit_bytes=VMEM_LIMIT),
        name="in_proj",
    )(x2, norm_w, w_in_bf, lb_logits, cos_t, sin_t)


def _hgrn_kernel(q_ref, k_ref, g_ref, v_ref, gate_ref, nw_ref, o_ref, st_ref):
    C, SB = HGRN_CHUNK, HGRN_SUB

    @pl.when(pl.program_id(0) == 0)
    def _():
        st_ref[...] = jnp.zeros_like(st_ref)

    ti = lax.broadcasted_iota(jnp.int32, (C, C), 0)
    si = lax.broadcasted_iota(jnp.int32, (C, C), 1)
    tri = (si <= ti).astype(F32)
    sub_t = lax.broadcasted_iota(jnp.int32, (SB, HEAD_W), 0)
    nw = nw_ref[...]

    def chunk(c, carry):
        r0 = pl.multiple_of(c * C, C)
        rows = pl.ds(r0, C)
        b_all = jnp.dot(tri, g_ref[rows, :], preferred_element_type=F32,
                        precision=lax.Precision.HIGHEST)
        for hh in range(N_HEADS):
            cols = slice(hh * HEAD_W, (hh + 1) * HEAD_W)
            b = b_all[:, cols]
            q = q_ref[rows, cols]
            k = k_ref[rows, cols]
            v = v_ref[rows, cols]
            v_bf = v.astype(BF16)
            st = st_ref[hh]
            inter = lax.dot_general((q * jnp.exp(b)).astype(BF16), st.astype(BF16),
                                    NT, preferred_element_type=F32)
            b_last = b[C - 1:C, :]
            k_dec = (k * jnp.exp(b_last - b)).astype(BF16)
            st_ref[hh] = st * jnp.exp(b_last) + lax.dot_general(
                v_bf, k_dec, TN, preferred_element_type=F32)

            outs = []
            for blk in range(C // SB):
                lo = blk * SB
                b_i = b[lo:lo + SB, :]
                q_i = q[lo:lo + SB, :]
                k_i = k[lo:lo + SB, :]
                v_i = v[lo:lo + SB, :]
                o_i = inter[lo:lo + SB, :]
                if blk > 0:
                    ref = b[lo - 1:lo, :]
                    q_t = (q_i * jnp.exp(b_i - ref)).astype(BF16)
                    k_t = (k[:lo, :] * jnp.exp(ref - b[:lo, :])).astype(BF16)
                    a = lax.dot_general(q_t, k_t, NT, preferred_element_type=F32)
                    o_i = o_i + jnp.dot(a.astype(BF16), v_bf[:lo, :],
                                        preferred_element_type=F32)
                qk_base = q_i
                for s in range(SB):
                    diff = jnp.where(sub_t >= s, b_i - b_i[s:s + 1, :], -jnp.inf)
                    w = qk_base * k_i[s:s + 1, :] * jnp.exp(diff)
                    o_i = o_i + jnp.sum(w, axis=-1, keepdims=True) * v_i[s:s + 1, :]
                outs.append(o_i)
            o = jnp.concatenate(outs, axis=0)
            ms = jnp.mean(o * o, axis=-1, keepdims=True)
            o_ref[rows, cols] = o * lax.rsqrt(ms + NORM_EPS) * nw * gate_ref[rows, cols]
        return carry

    lax.fori_loop(0, HGRN_TM // C, chunk, 0)


def _hgrn(hq, hk, hgl, hi, hgate, norm_w):
    T = hq.shape[0]
    blk = pl.BlockSpec((HGRN_TM, GROUP_W), lambda i: (i, 0))
    return pl.pallas_call(
        _hgrn_kernel,
        grid=(T // HGRN_TM,),
        in_specs=[blk, blk, blk, blk, blk, pl.BlockSpec((1, HEAD_W), lambda i: (0, 0))],
        out_specs=blk,
        out_shape=jax.ShapeDtypeStruct((T, GROUP_W), F32),
        scratch_shapes=[pltpu.VMEM((N_HEADS, HEAD_W, HEAD_W), F32)],
        compiler_params=pltpu.CompilerParams(
            dimension_semantics=("arbitrary",), vmem_limit_bytes=VMEM_LIMIT),
        name="hgrn2",
    )(hq, hk, hgl, hi, hgate, norm_w)


def _attn_kernel(lq1_ref, lk1_ref, lq2_ref, lk2_ref, q_ref, k_ref, vt_ref, gate_ref,
                 nw_ref, o_ref, s_ref, p_ref, alpha_ref, acc_ref, *, lambda_init):
    qi = pl.program_id(1)
    lam = (jnp.exp(jnp.sum(lq1_ref[...] * lk1_ref[...]))
           - jnp.exp(jnp.sum(lq2_ref[...] * lk2_ref[...])) + lambda_init)

    q = q_ref[...]
    lane = lax.broadcasted_iota(jnp.int32, q.shape, 1)
    zero = jnp.zeros_like(q)
    q_maps = (jnp.where(lane < QK_DIM, q, zero), jnp.where(lane >= QK_DIM, q, zero))

    n_t = ATT_TQ // ATT_TN
    acc_ref[...] = jnp.zeros_like(acc_ref)

    def scores(t, slot):
        k0 = pl.multiple_of(t * ATT_TK, ATT_TK)
        kb = k_ref[pl.ds(k0, ATT_TK), :]
        mx = []
        for c in range(2):
            for n in range(n_t):
                cols = slice(n * ATT_TN, (n + 1) * ATT_TN)
                s = lax.dot_general(kb, q_maps[c][cols, :], NT,
                                    preferred_element_type=F32)
                s_ref[slot, c, :, cols] = s
                mx.append(jnp.max(s, axis=0, keepdims=True))
        return tuple(mx)

    def softmax(slot, mx, m_old, masked):
        m_out = []
        for c in range(2):
            for n in range(n_t):
                i = c * n_t + n
                cols = slice(n * ATT_TN, (n + 1) * ATT_TN)
                s = s_ref[slot, c, :, cols]
                if masked:
                    kpos = lax.broadcasted_iota(jnp.int32, s.shape, 0)
                    qpos = lax.broadcasted_iota(jnp.int32, s.shape, 1) + n * ATT_TN
                    s = jnp.where(kpos <= qpos, s, -jnp.inf)
                    blk_max = jnp.max(s, axis=0, keepdims=True)
                else:
                    blk_max = mx[i]
                m_new = jnp.maximum(m_old[i], blk_max)
                alpha_ref[slot, c, :, cols] = jnp.exp2(m_old[i] - m_new)
                p_ref[slot, c, :, cols] = jnp.exp2(s - m_new).astype(BF16)
                m_out.append(m_new)
        return tuple(m_out)

    def pv(t, slot):
        k0 = pl.multiple_of(t * ATT_TK, ATT_TK)
        vt = vt_ref[:, pl.ds(k0, ATT_TK)]
        for c in range(2):
            upd = jnp.dot(vt, p_ref[slot, c], preferred_element_type=F32)
            acc_ref[c] = alpha_ref[slot, c] * acc_ref[c] + upd

    def step(t, slot, carry):
        mx, m_old = carry
        mx_next = scores(t, slot)
        m_new = softmax(1 - slot, mx, m_old, False)
        pv(t - 2, slot)
        return mx_next, m_new

    m0 = tuple(jnp.full((1, ATT_TN), -jnp.inf, F32) for _ in range(2 * n_t))
    mx0 = scores(0, 0)

    def warmup(_):
        mx1 = scores(1, 1)
        return mx1, softmax(0, mx0, m0, False)

    carry = lax.cond(qi >= 1, warmup, lambda _: (mx0, m0), 0)

    def pair(i, carry):
        t = 2 + 2 * i
        return step(t + 1, 1, step(t, 0, carry))

    n_steps = jnp.maximum(qi - 1, 0)
    carry = lax.fori_loop(0, n_steps // 2, pair, carry)
    carry = lax.cond(n_steps % 2 == 1, lambda c: step(qi, 0, c), lambda c: c, carry)
    mx, m_run = carry

    @pl.when(qi % 2 == 0)
    def _():
        @pl.when(qi >= 2)
        def _():
            pv(qi - 1, 1)
        softmax(0, mx, m_run, True)
        pv(qi, 0)

    @pl.when(qi % 2 == 1)
    def _():
        pv(qi - 1, 0)
        softmax(1, mx, m_run, True)
        pv(qi, 1)

    a1 = acc_ref[0]
    a2 = acc_ref[1]
    o_t = (a1[:HEAD_W] / a1[HEAD_W:HEAD_W + 1]
           - lam * (a2[:HEAD_W] / a2[HEAD_W:HEAD_W + 1]))
    o = o_t.T
    ms = jnp.mean(o * o, axis=-1, keepdims=True)
    o = o * lax.rsqrt(ms + SUBLN_EPS) * nw_ref[...] * (1.0 - lambda_init)
    o_ref[...] = o * gate_ref[...]


def _diff_attn(lq1, lk1, lq2, lk2, dq, dk, dvt, dgate, norm_w, lambda_init):
    T = dq.shape[0]
    lam_spec = pl.BlockSpec((1, QK_DIM), lambda h, i: (0, 0))
    return pl.pallas_call(
        functools.partial(_attn_kernel, lambda_init=lambda_init),
        grid=(N_HEADS, T // ATT_TQ),
        in_specs=[
            lam_spec, lam_spec, lam_spec, lam_spec,
            pl.BlockSpec((ATT_TQ, HEAD_W), lambda h, i: (i, h)),
            pl.BlockSpec((T, HEAD_W), lambda h, i: (0, h)),
            pl.BlockSpec((None, VT_ROWS, T), lambda h, i: (h, 0, 0)),
            pl.BlockSpec((ATT_TQ, HEAD_W), lambda h, i: (i, h)),
            pl.BlockSpec((1, HEAD_W), lambda h, i: (0, 0)),
        ],
        out_specs=pl.BlockSpec((ATT_TQ, HEAD_W), lambda h, i: (i, h)),
        out_shape=jax.ShapeDtypeStruct((T, GROUP_W), F32),
        scratch_shapes=[
            pltpu.VMEM((2, 2, ATT_TK, ATT_TQ), F32),
            pltpu.VMEM((2, 2, ATT_TK, ATT_TQ), BF16),
            pltpu.VMEM((2, 2, 1, ATT_TQ), F32),
            pltpu.VMEM((2, VT_ROWS, ATT_TQ), F32),
        ],
        compiler_params=pltpu.CompilerParams(
            dimension_semantics=("arbitrary", "arbitrary"), vmem_limit_bytes=VMEM_LIMIT),
        name="diff_attn",
    )(lq1, lk1, lq2, lk2, dq, dk, dvt, dgate, norm_w)


def _out_proj_kernel(ho_ref, do_ref, x_ref, w_ref, nw_ref, o_ref):
    y = x_ref[...]
    y = y + jnp.dot(ho_ref[...].astype(BF16), w_ref[:GROUP_W, :], preferred_element_type=F32)
    y = y + jnp.dot(do_ref[...].astype(BF16), w_ref[GROUP_W:, :], preferred_element_type=F32)
    ms = jnp.mean(y * y, axis=-1, keepdims=True)
    o_ref[...] = y * lax.rsqrt(ms + NORM_EPS) * nw_ref[...]


def _out_proj(ho, do, x2, w_out_bf, final_norm_w):
    T, D = x2.shape
    row = lambda i: (i, 0)
    fix = lambda i: (0, 0)
    return pl.pallas_call(
        _out_proj_kernel,
        grid=(T // PROJ_TM,),
        in_specs=[
            pl.BlockSpec((PROJ_TM, GROUP_W), row),
            pl.BlockSpec((PROJ_TM, GROUP_W), row),
            pl.BlockSpec((PROJ_TM, D), row),
            pl.BlockSpec(w_out_bf.shape, fix),
            pl.BlockSpec((1, D), fix),
        ],
        out_specs=pl.BlockSpec((PROJ_TM, D), row),
        out_shape=jax.ShapeDtypeStruct((T, D), F32),
        compiler_params=pltpu.CompilerParams(
            dimension_semantics=("arbitrary",), vmem_limit_bytes=VMEM_LIMIT),
        name="out_proj",
    )(ho, do, x2, w_out_bf, final_norm_w)


def _rope_tables(T):
    half = QK_DIM // 2
    inv_freq = 1.0 / (ROPE_THETA ** (jnp.arange(half, dtype=F32) / half))
    ang = jnp.arange(T, dtype=F32)[:, None] * inv_freq[None, :]
    cos, sin = jnp.cos(ang), jnp.sin(ang)
    reps = LANES // QK_DIM
    cos_t = jnp.tile(jnp.concatenate([cos, cos], axis=1), (1, reps))
    sin_t = jnp.tile(jnp.concatenate([-sin, sin], axis=1), (1, reps))
    return cos_t, sin_t


def kernel(x, norm_w, w_in, hgrn_lb_logits, hgrn_norm_w, diff_lambda_q1, diff_lambda_k1,
           diff_lambda_q2, diff_lambda_k2, diff_norm_w, w_out, final_norm_w):
    B, T, D = x.shape
    depth = norm_w.shape[0]
    assert B == 1 and depth == 1 and D == 2 * GROUP_W
    assert T % ATT_TQ == 0 and T % HGRN_TM == 0 and T % PROJ_TM == 0
    x2 = x.reshape(T, D)
    cos_t, sin_t = _rope_tables(T)
    l = 0
    lambda_init = 0.8 - 0.6 * math.exp(-0.3 * l)
    hq, hk, hgl, hi, hgate, dq, dk, dvt, dgate = _in_proj(
        x2, norm_w[l:l + 1], w_in[l].astype(BF16), hgrn_lb_logits, cos_t, sin_t)
    ho = _hgrn(hq, hk, hgl, hi, hgate, hgrn_norm_w[l:l + 1])
    do = _diff_attn(diff_lambda_q1[l:l + 1], diff_lambda_k1[l:l + 1],
                    diff_lambda_q2[l:l + 1], diff_lambda_k2[l:l + 1],
                    dq, dk, dvt, dgate, diff_norm_w[l:l + 1], lambda_init)
    out = _out_proj(ho, do, x2, w_out[l].astype(BF16), final_norm_w.reshape(1, D))
    return out.reshape(B, T, D)
```

```python
import functools
import math

import jax
import jax.numpy as jnp
from jax import lax
from jax.experimental import pallas as pl
from jax.experimental.pallas import tpu as pltpu

F32 = jnp.float32
BF16 = jnp.bfloat16

LANES = 128
SUBLANES = 8
VMEM_LIMIT = 56 * 1024 * 1024

N_HEADS = 4
HEAD_W = 128
GROUP_W = N_HEADS * HEAD_W
QK_DIM = 64
ROPE_THETA = 10000.0
NORM_EPS = 1e-6
SUBLN_EPS = 1e-5

PROJ_TM = 512
HGRN_CHUNK = 64
HGRN_SUB = 16
HGRN_TM = 512
HGRN_UNROLL = 4
ATT_TQ = 512
ATT_TK = 512
ATT_TN = 256
VT_ROWS = HEAD_W + 16
LOG2E = math.log2(math.e)

NT = (((1,), (1,)), ((), ()))
TN = (((0,), (0,)), ((), ()))


def _silu(x):
    return x * (1.0 / (1.0 + jnp.exp(-x)))


def _in_proj_kernel(x_ref, nw_ref, w_ref, lbl_ref, cos_ref, sin_ref,
                    hq_ref, hk_ref, hgl_ref, hi_ref, hgate_ref,
                    dq_ref, dk_ref, dvt_ref, dgate_ref):
    x = x_ref[...]
    ms = jnp.mean(x * x, axis=-1, keepdims=True)
    h = (x * lax.rsqrt(ms + NORM_EPS) * nw_ref[...]).astype(BF16)

    def seg(i):
        return jnp.dot(h, w_ref[:, i * GROUP_W:(i + 1) * GROUP_W],
                       preferred_element_type=F32)

    lg = lbl_ref[...]
    e = jnp.exp(lg - jnp.max(lg, axis=0, keepdims=True))
    lb = e[0:1, :] / jnp.sum(e, axis=0, keepdims=True)

    hq_ref[...] = seg(0)
    f = lb + (1.0 - lb) * (1.0 / (1.0 + jnp.exp(-seg(1))))
    hk_ref[...] = 1.0 - f
    hgl_ref[...] = jnp.log2(f)
    hi_ref[...] = seg(2)
    hgate_ref[...] = _silu(seg(3))

    cos = cos_ref[...]
    sin = sin_ref[...]
    lane = lax.broadcasted_iota(jnp.int32, cos.shape, 1)
    first_half = (lane % QK_DIM) < (QK_DIM // 2)

    def rope(t, scale):
        outs = []
        for hh in range(N_HEADS):
            th = t[:, hh * HEAD_W:(hh + 1) * HEAD_W]
            up = pltpu.roll(th, HEAD_W - QK_DIM // 2, 1)
            dn = pltpu.roll(th, QK_DIM // 2, 1)
            partner = jnp.where(first_half, up, dn)
            outs.append(((th * cos + partner * sin) * scale).astype(BF16))
        return jnp.concatenate(outs, axis=1)

    dq_ref[...] = rope(seg(4), QK_DIM ** -0.5 * LOG2E)
    dk_ref[...] = rope(seg(5), 1.0)
    vt = seg(6).T.astype(BF16)
    for hh in range(N_HEADS):
        dvt_ref[hh, :HEAD_W, :] = vt[hh * HEAD_W:(hh + 1) * HEAD_W, :]
        dvt_ref[hh, HEAD_W:, :] = jnp.ones((VT_ROWS - HEAD_W, PROJ_TM), BF16)
    dgate_ref[...] = _silu(seg(7))


def _in_proj(x2, norm_w, w_in_bf, lb_logits, cos_t, sin_t):
    T, D = x2.shape
    n = T // PROJ_TM
    row = lambda i: (i, 0)
    fix = lambda i: (0, 0)
    f32_out = jax.ShapeDtypeStruct((T, GROUP_W), F32)
    bf_out = jax.ShapeDtypeStruct((T, GROUP_W), BF16)
    blk = pl.BlockSpec((PROJ_TM, GROUP_W), row)
    return pl.pallas_call(
        _in_proj_kernel,
        grid=(n,),
        in_specs=[
            pl.BlockSpec((PROJ_TM, D), row),
            pl.BlockSpec((1, D), fix),
            pl.BlockSpec(w_in_bf.shape, fix),
            pl.BlockSpec(lb_logits.shape, fix),
            pl.BlockSpec((PROJ_TM, LANES), row),
            pl.BlockSpec((PROJ_TM, LANES), row),
        ],
        out_specs=[blk, blk, blk, blk, blk, blk, blk,
                   pl.BlockSpec((N_HEADS, VT_ROWS, PROJ_TM), lambda i: (0, 0, i)), blk],
        out_shape=[f32_out, f32_out, f32_out, f32_out, f32_out, bf_out, bf_out,
                   jax.ShapeDtypeStruct((N_HEADS, VT_ROWS, T), BF16), f32_out],
        compiler_params=pltpu.CompilerParams(
            dimension_semantics=("arbitrary",), vmem_limit_bytes=VMEM_LIMIT),
        name="in_proj",
    )(x2, norm_w, w_in_bf, lb_logits, cos_t, sin_t)


def _hgrn_kernel(q_ref, k_ref, g_ref, v_ref, gate_ref, nw_ref, o_ref, st_ref, c_ref):
    C, SB = HGRN_CHUNK, HGRN_SUB

    @pl.when(pl.program_id(0) == 0)
    def _():
        st_ref[...] = jnp.zeros_like(st_ref)

    ti = lax.broadcasted_iota(jnp.int32, (C, C), 0)
    si = lax.broadcasted_iota(jnp.int32, (C, C), 1)
    tri = (si <= ti).astype(F32)
    a_row = lax.broadcasted_iota(jnp.int32, (SUBLANES, C), 0)
    a_lane = lax.broadcasted_iota(jnp.int32, (SUBLANES, C), 1)
    nw = nw_ref[...]

    def chunk(c, carry):
        r0 = pl.multiple_of(c * C, C)
        rows = pl.ds(r0, C)
        b_all = jnp.dot(tri, g_ref[rows, :], preferred_element_type=F32,
                        precision=lax.Precision.HIGHEST)
        c_ref[...] = b_all - jnp.log2(jnp.maximum(k_ref[rows, :], 0.0))
        for hh in range(N_HEADS):
            cols = slice(hh * HEAD_W, (hh + 1) * HEAD_W)
            b = b_all[:, cols]
            q = q_ref[rows, cols]
            k = k_ref[rows, cols]
            v = v_ref[rows, cols]
            v_bf = v.astype(BF16)
            st = st_ref[hh]
            inter = lax.dot_general((q * jnp.exp2(b)).astype(BF16), st.astype(BF16),
                                    NT, preferred_element_type=F32)
            b_last = b[C - 1:C, :]
            k_dec = (k * jnp.exp2(b_last - b)).astype(BF16)
            st_ref[hh] = st * jnp.exp2(b_last) + lax.dot_general(
                v_bf, k_dec, TN, preferred_element_type=F32)

            pieces = []
            for blk in range(C // SB):
                lo = blk * SB
                if blk > 0:
                    ref = b[lo - 1:lo, :]
                    q_t = (q[lo:lo + SB, :] * jnp.exp2(b[lo:lo + SB, :] - ref)).astype(BF16)
                    k_t = (k[:lo, :] * jnp.exp2(ref - b[:lo, :])).astype(BF16)
                    k_t = jnp.concatenate([k_t, jnp.zeros((C - lo, HEAD_W), BF16)], axis=0)
                    a_far = lax.dot_general(q_t, k_t, NT, preferred_element_type=F32)
                    halves = [a_far[:SUBLANES, :], a_far[SUBLANES:, :]]
                else:
                    halves = [jnp.zeros((SUBLANES, C), F32), jnp.zeros((SUBLANES, C), F32)]
                for s in range(SB):
                    c_s = c_ref[pl.ds(lo + s, 1), cols]
                    for half in range(s // SUBLANES, 2):
                        r = lo + half * SUBLANES
                        w = q[r:r + SUBLANES, :] * jnp.exp2(b[r:r + SUBLANES, :] - c_s)
                        col = jnp.sum(w, axis=-1, keepdims=True)
                        halves[half] = jnp.where(a_lane == lo + s, col, halves[half])
                for half in range(2):
                    keep = a_lane <= a_row + (lo + half * SUBLANES)
                    pieces.append(jnp.where(keep, halves[half], 0.0))
            a = jnp.concatenate(pieces, axis=0).astype(BF16)
            o = inter + jnp.dot(a, v_bf, preferred_element_type=F32)
            ms = jnp.mean(o * o, axis=-1, keepdims=True)
            o_ref[rows, cols] = o * lax.rsqrt(ms + NORM_EPS) * nw * gate_ref[rows, cols]
        return carry

    lax.fori_loop(0, HGRN_TM // C, chunk, 0, unroll=HGRN_UNROLL)


def _hgrn(hq, hk, hgl, hi, hgate, norm_w):
    T = hq.shape[0]
    blk = pl.BlockSpec((HGRN_TM, GROUP_W), lambda i: (i, 0))
    return pl.pallas_call(
        _hgrn_kernel,
        grid=(T // HGRN_TM,),
        in_specs=[blk, blk, blk, blk, blk, pl.BlockSpec((1, HEAD_W), lambda i: (0, 0))],
        out_specs=blk,
        out_shape=jax.ShapeDtypeStruct((T, GROUP_W), F32),
        scratch_shapes=[pltpu.VMEM((N_HEADS, HEAD_W, HEAD_W), F32),
                        pltpu.VMEM((HGRN_CHUNK, GROUP_W), F32)],
        compiler_params=pltpu.CompilerParams(
            dimension_semantics=("arbitrary",), vmem_limit_bytes=VMEM_LIMIT),
        name="hgrn2",
    )(hq, hk, hgl, hi, hgate, norm_w)


def _attn_kernel(lq1_ref, lk1_ref, lq2_ref, lk2_ref, q_ref, k_ref, vt_ref, gate_ref,
                 nw_ref, o_ref, s_ref, p_ref, alpha_ref, acc_ref, *, lambda_init):
    qi = pl.program_id(1)
    lam = (jnp.exp(jnp.sum(lq1_ref[...] * lk1_ref[...]))
           - jnp.exp(jnp.sum(lq2_ref[...] * lk2_ref[...])) + lambda_init)

    q = q_ref[...]
    lane = lax.broadcasted_iota(jnp.int32, q.shape, 1)
    zero = jnp.zeros_like(q)
    q_maps = (jnp.where(lane < QK_DIM, q, zero), jnp.where(lane >= QK_DIM, q, zero))

    n_t = ATT_TQ // ATT_TN
    acc_ref[...] = jnp.zeros_like(acc_ref)

    def scores(t, slot):
        k0 = pl.multiple_of(t * ATT_TK, ATT_TK)
        kb = k_ref[pl.ds(k0, ATT_TK), :]
        mx = []
        for c in range(2):
            for n in range(n_t):
                cols = slice(n * ATT_TN, (n + 1) * ATT_TN)
                s = lax.dot_general(kb, q_maps[c][cols, :], NT,
                                    preferred_element_type=F32)
                s_ref[slot, c, :, cols] = s
                mx.append(jnp.max(s, axis=0, keepdims=True))
        return tuple(mx)

    def softmax(slot, mx, m_old, masked):
        m_out = []
        for c in range(2):
            for n in range(n_t):
                i = c * n_t + n
                cols = slice(n * ATT_TN, (n + 1) * ATT_TN)
                s = s_ref[slot, c, :, cols]
                if masked:
                    kpos = lax.broadcasted_iota(jnp.int32, s.shape, 0)
                    qpos = lax.broadcasted_iota(jnp.int32, s.shape, 1) + n * ATT_TN
                    s = jnp.where(kpos <= qpos, s, -jnp.inf)
                    blk_max = jnp.max(s, axis=0, keepdims=True)
                else:
                    blk_max = mx[i]
                m_new = jnp.maximum(m_old[i], blk_max)
                alpha_ref[slot, c, :, cols] = jnp.exp2(m_old[i] - m_new)
                p_ref[slot, c, :, cols] = jnp.exp2(s - m_new).astype(BF16)
                m_out.append(m_new)
        return tuple(m_out)

    def pv(t, slot):
        k0 = pl.multiple_of(t * ATT_TK, ATT_TK)
        vt = vt_ref[:, pl.ds(k0, ATT_TK)]
        for c in range(2):
            upd = jnp.dot(vt, p_ref[slot, c], preferred_element_type=F32)
            acc_ref[c] = alpha_ref[slot, c] * acc_ref[c] + upd

    def step(t, slot, carry):
        mx, m_old = carry
        mx_next = scores(t, slot)
        m_new = softmax(1 - slot, mx, m_old, False)
        pv(t - 2, slot)
        return mx_next, m_new

    m0 = tuple(jnp.full((1, ATT_TN), -jnp.inf, F32) for _ in range(2 * n_t))
    mx0 = scores(0, 0)

    def warmup(_):
        mx1 = scores(1, 1)
        return mx1, softmax(0, mx0, m0, False)

    carry = lax.cond(qi >= 1, warmup, lambda _: (mx0, m0), 0)

    def pair(i, carry):
        t = 2 + 2 * i
        return step(t + 1, 1, step(t, 0, carry))

    n_steps = jnp.maximum(qi - 1, 0)
    carry = lax.fori_loop(0, n_steps // 2, pair, carry)
    carry = lax.cond(n_steps % 2 == 1, lambda c: step(qi, 0, c), lambda c: c, carry)
    mx, m_run = carry

    @pl.when(qi % 2 == 0)
    def _():
        @pl.when(qi >= 2)
        def _():
            pv(qi - 1, 1)
        softmax(0, mx, m_run, True)
        pv(qi, 0)

    @pl.when(qi % 2 == 1)
    def _():
        pv(qi - 1, 0)
        softmax(1, mx, m_run, True)
        pv(qi, 1)

    a1 = acc_ref[0]
    a2 = acc_ref[1]
    o_t = (a1[:HEAD_W] / a1[HEAD_W:HEAD_W + 1]
           - lam * (a2[:HEAD_W] / a2[HEAD_W:HEAD_W + 1]))
    o = o_t.T
    ms = jnp.mean(o * o, axis=-1, keepdims=True)
    o = o * lax.rsqrt(ms + SUBLN_EPS) * nw_ref[...] * (1.0 - lambda_init)
    o_ref[...] = o * gate_ref[...]


def _diff_attn(lq1, lk1, lq2, lk2, dq, dk, dvt, dgate, norm_w, lambda_init):
    T = dq.shape[0]
    lam_spec = pl.BlockSpec((1, QK_DIM), lambda h, i: (0, 0))
    return pl.pallas_call(
        functools.partial(_attn_kernel, lambda_init=lambda_init),
        grid=(N_HEADS, T // ATT_TQ),
        in_specs=[
            lam_spec, lam_spec, lam_spec, lam_spec,
            pl.BlockSpec((ATT_TQ, HEAD_W), lambda h, i: (i, h)),
            pl.BlockSpec((T, HEAD_W), lambda h, i: (0, h)),
            pl.BlockSpec((None, VT_ROWS, T), lambda h, i: (h, 0, 0)),
            pl.BlockSpec((ATT_TQ, HEAD_W), lambda h, i: (i, h)),
            pl.BlockSpec((1, HEAD_W), lambda h, i: (0, 0)),
        ],
        out_specs=pl.BlockSpec((ATT_TQ, HEAD_W), lambda h, i: (i, h)),
        out_shape=jax.ShapeDtypeStruct((T, GROUP_W), F32),
        scratch_shapes=[
            pltpu.VMEM((2, 2, ATT_TK, ATT_TQ), F32),
            pltpu.VMEM((2, 2, ATT_TK, ATT_TQ), BF16),
            pltpu.VMEM((2, 2, 1, ATT_TQ), F32),
            pltpu.VMEM((2, VT_ROWS, ATT_TQ), F32),
        ],
        compiler_params=pltpu.CompilerParams(
            dimension_semantics=("arbitrary", "arbitrary"), vmem_limit_bytes=VMEM_LIMIT),
        name="diff_attn",
    )(lq1, lk1, lq2, lk2, dq, dk, dvt, dgate, norm_w)


def _out_proj_kernel(ho_ref, do_ref, x_ref, w_ref, nw_ref, o_ref):
    y = x_ref[...]
    y = y + jnp.dot(ho_ref[...].astype(BF16), w_ref[:GROUP_W, :], preferred_element_type=F32)
    y = y + jnp.dot(do_ref[...].astype(BF16), w_ref[GROUP_W:, :], preferred_element_type=F32)
    ms = jnp.mean(y * y, axis=-1, keepdims=True)
    o_ref[...] = y * lax.rsqrt(ms + NORM_EPS) * nw_ref[...]


def _out_proj(ho, do, x2, w_out_bf, final_norm_w):
    T, D = x2.shape
    row = lambda i: (i, 0)
    fix = lambda i: (0, 0)
    return pl.pallas_call(
        _out_proj_kernel,
        grid=(T // PROJ_TM,),
        in_specs=[
            pl.BlockSpec((PROJ_TM, GROUP_W), row),
            pl.BlockSpec((PROJ_TM, GROUP_W), row),
            pl.BlockSpec((PROJ_TM, D), row),
            pl.BlockSpec(w_out_bf.shape, fix),
            pl.BlockSpec((1, D), fix),
        ],
        out_specs=pl.BlockSpec((PROJ_TM, D), row),
        out_shape=jax.ShapeDtypeStruct((T, D), F32),
        compiler_params=pltpu.CompilerParams(
            dimension_semantics=("arbitrary",), vmem_limit_bytes=VMEM_LIMIT),
        name="out_proj",
    )(ho, do, x2, w_out_bf, final_norm_w)


def _rope_tables(T):
    half = QK_DIM // 2
    inv_freq = 1.0 / (ROPE_THETA ** (jnp.arange(half, dtype=F32) / half))
    ang = jnp.arange(T, dtype=F32)[:, None] * inv_freq[None, :]
    cos, sin = jnp.cos(ang), jnp.sin(ang)
    reps = LANES // QK_DIM
    cos_t = jnp.tile(jnp.concatenate([cos, cos], axis=1), (1, reps))
    sin_t = jnp.tile(jnp.concatenate([-sin, sin], axis=1), (1, reps))
    return cos_t, sin_t


def kernel(x, norm_w, w_in, hgrn_lb_logits, hgrn_norm_w, diff_lambda_q1, diff_lambda_k1,
           diff_lambda_q2, diff_lambda_k2, diff_norm_w, w_out, final_norm_w):
    B, T, D = x.shape
    depth = norm_w.shape[0]
    assert B == 1 and depth == 1 and D == 2 * GROUP_W
    assert T % ATT_TQ == 0 and T % HGRN_TM == 0 and T % PROJ_TM == 0
    x2 = x.reshape(T, D)
    cos_t, sin_t = _rope_tables(T)
    l = 0
    lambda_init = 0.8 - 0.6 * math.exp(-0.3 * l)
    hq, hk, hgl, hi, hgate, dq, dk, dvt, dgate = _in_proj(
        x2, norm_w[l:l + 1], w_in[l].astype(BF16), hgrn_lb_logits, cos_t, sin_t)
    ho = _hgrn(hq, hk, hgl, hi, hgate, hgrn_norm_w[l:l + 1])
    do = _diff_attn(diff_lambda_q1[l:l + 1], diff_lambda_k1[l:l + 1],
                    diff_lambda_q2[l:l + 1], diff_lambda_k2[l:l + 1],
                    dq, dk, dvt, dgate, diff_norm_w[l:l + 1], lambda_init)
    out = _out_proj(ho, do, x2, w_out[l].astype(BF16), final_norm_w.reshape(1, D))
    return out.reshape(B, T, D)
```

```python
import functools
import math

import jax
import jax.numpy as jnp
from jax import lax
from jax.experimental import pallas as pl
from jax.experimental.pallas import tpu as pltpu

F32 = jnp.float32
BF16 = jnp.bfloat16

LANES = 128
SUBLANES = 8
VMEM_LIMIT = 56 * 1024 * 1024

N_HEADS = 4
HEAD_W = 128
GROUP_W = N_HEADS * HEAD_W
QK_DIM = 64
ROPE_THETA = 10000.0
NORM_EPS = 1e-6
SUBLN_EPS = 1e-5

PROJ_TM = 512
HGRN_CHUNK = 64
HGRN_SUB = 16
HGRN_TM = 512
HGRN_UNROLL = 4
ATT_TQ = 512
ATT_TK = 512
ATT_TN = 256
VT_ROWS = HEAD_W + 16
LOG2E = math.log2(math.e)

NT = (((1,), (1,)), ((), ()))
TN = (((0,), (0,)), ((), ()))


def _silu(x):
    return x * (1.0 / (1.0 + jnp.exp(-x)))


def _in_proj_kernel(x_ref, nw_ref, w_ref, lbl_ref, cos_ref, sin_ref,
                    hq_ref, hk_ref, hgl_ref, hi_ref, hgate_ref,
                    dq_ref, dk_ref, dvt_ref, dgate_ref):
    x = x_ref[...]
    ms = jnp.mean(x * x, axis=-1, keepdims=True)
    h = (x * lax.rsqrt(ms + NORM_EPS) * nw_ref[...]).astype(BF16)

    def seg(i):
        return jnp.dot(h, w_ref[:, i * GROUP_W:(i + 1) * GROUP_W],
                       preferred_element_type=F32)

    lg = lbl_ref[...]
    e = jnp.exp(lg - jnp.max(lg, axis=0, keepdims=True))
    lb = e[0:1, :] / jnp.sum(e, axis=0, keepdims=True)

    hq_ref[...] = seg(0)
    f = lb + (1.0 - lb) * (1.0 / (1.0 + jnp.exp(-seg(1))))
    hk_ref[...] = 1.0 - f
    hgl_ref[...] = jnp.log2(f)
    hi_ref[...] = seg(2)
    hgate_ref[...] = _silu(seg(3))

    cos = cos_ref[...]
    sin = sin_ref[...]
    lane = lax.broadcasted_iota(jnp.int32, cos.shape, 1)
    first_half = (lane % QK_DIM) < (QK_DIM // 2)

    def rope(t, scale):
        outs = []
        for hh in range(N_HEADS):
            th = t[:, hh * HEAD_W:(hh + 1) * HEAD_W]
            up = pltpu.roll(th, HEAD_W - QK_DIM // 2, 1)
            dn = pltpu.roll(th, QK_DIM // 2, 1)
            partner = jnp.where(first_half, up, dn)
            outs.append(((th * cos + partner * sin) * scale).astype(BF16))
        return jnp.concatenate(outs, axis=1)

    dq_ref[...] = rope(seg(4), QK_DIM ** -0.5 * LOG2E)
    dk_ref[...] = rope(seg(5), 1.0)
    vt = seg(6).T.astype(BF16)
    for hh in range(N_HEADS):
        dvt_ref[hh, :HEAD_W, :] = vt[hh * HEAD_W:(hh + 1) * HEAD_W, :]
        dvt_ref[hh, HEAD_W:, :] = jnp.ones((VT_ROWS - HEAD_W, PROJ_TM), BF16)
    dgate_ref[...] = _silu(seg(7))


def _in_proj(x2, norm_w, w_in_bf, lb_logits, cos_t, sin_t):
    T, D = x2.shape
    n = T // PROJ_TM
    row = lambda i: (i, 0)
    fix = lambda i: (0, 0)
    f32_out = jax.ShapeDtypeStruct((T, GROUP_W), F32)
    bf_out = jax.ShapeDtypeStruct((T, GROUP_W), BF16)
    blk = pl.BlockSpec((PROJ_TM, GROUP_W), row)
    return pl.pallas_call(
        _in_proj_kernel,
        grid=(n,),
        in_specs=[
            pl.BlockSpec((PROJ_TM, D), row),
            pl.BlockSpec((1, D), fix),
            pl.BlockSpec(w_in_bf.shape, fix),
            pl.BlockSpec(lb_logits.shape, fix),
            pl.BlockSpec((PROJ_TM, LANES), row),
            pl.BlockSpec((PROJ_TM, LANES), row),
        ],
        out_specs=[blk, blk, blk, blk, blk, blk, blk,
                   pl.BlockSpec((N_HEADS, VT_ROWS, PROJ_TM), lambda i: (0, 0, i)), blk],
        out_shape=[f32_out, f32_out, f32_out, f32_out, f32_out, bf_out, bf_out,
                   jax.ShapeDtypeStruct((N_HEADS, VT_ROWS, T), BF16), f32_out],
        compiler_params=pltpu.CompilerParams(
            dimension_semantics=("arbitrary",), vmem_limit_bytes=VMEM_LIMIT),
        name="in_proj",
    )(x2, norm_w, w_in_bf, lb_logits, cos_t, sin_t)


def _hgrn_kernel(q_ref, k_ref, g_ref, v_ref, gate_ref, nw_ref, o_ref, st_ref, c_ref):
    C, SB = HGRN_CHUNK, HGRN_SUB

    @pl.when(pl.program_id(0) == 0)
    def _():
        st_ref[...] = jnp.zeros_like(st_ref)

    ti = lax.broadcasted_iota(jnp.int32, (C, C), 0)
    si = lax.broadcasted_iota(jnp.int32, (C, C), 1)
    tri = (si <= ti).astype(F32)
    a_row = lax.broadcasted_iota(jnp.int32, (SUBLANES, C), 0)
    a_lane = lax.broadcasted_iota(jnp.int32, (SUBLANES, C), 1)
    nw = nw_ref[...]

    def chunk(c, carry):
        r0 = pl.multiple_of(c * C, C)
        rows = pl.ds(r0, C)
        b_all = jnp.dot(tri, g_ref[rows, :], preferred_element_type=F32,
                        precision=lax.Precision.HIGHEST)
        c_ref[...] = b_all - jnp.log2(jnp.maximum(k_ref[rows, :], 0.0))
        for hh in range(N_HEADS):
            cols = slice(hh * HEAD_W, (hh + 1) * HEAD_W)
            b = b_all[:, cols]
            q = q_ref[rows, cols]
            k = k_ref[rows, cols]
            v = v_ref[rows, cols]
            v_bf = v.astype(BF16)
            st = st_ref[hh]
            inter = lax.dot_general((q * jnp.exp2(b)).astype(BF16), st.astype(BF16),
                                    NT, preferred_element_type=F32)
            b_last = b[C - 1:C, :]
            k_dec = (k * jnp.exp2(b_last - b)).astype(BF16)
            st_ref[hh] = st * jnp.exp2(b_last) + lax.dot_general(
                v_bf, k_dec, TN, preferred_element_type=F32)

            pieces = []
            for blk in range(C // SB):
                lo = blk * SB
                if blk > 0:
                    ref = b[lo - 1:lo, :]
                    q_t = (q[lo:lo + SB, :] * jnp.exp2(b[lo:lo + SB, :] - ref)).astype(BF16)
                    k_t = (k[:lo, :] * jnp.exp2(ref - b[:lo, :])).astype(BF16)
                    k_t = jnp.concatenate([k_t, jnp.zeros((C - lo, HEAD_W), BF16)], axis=0)
                    a_far = lax.dot_general(q_t, k_t, NT, preferred_element_type=F32)
                    halves = [a_far[:SUBLANES, :], a_far[SUBLANES:, :]]
                else:
                    halves = [jnp.zeros((SUBLANES, C), F32), jnp.zeros((SUBLANES, C), F32)]
                for s in range(SB):
                    c_s = c_ref[pl.ds(lo + s, 1), cols]
                    for half in range(s // SUBLANES, 2):
                        r = lo + half * SUBLANES
                        w = q[r:r + SUBLANES, :] * jnp.exp2(b[r:r + SUBLANES, :] - c_s)
                        col = jnp.sum(w, axis=-1, keepdims=True)
                        halves[half] = jnp.where(a_lane == lo + s, col, halves[half])
                for half in range(2):
                    keep = a_lane <= a_row + (lo + half * SUBLANES)
                    pieces.append(jnp.where(keep, halves[half], 0.0))
            a = jnp.concatenate(pieces, axis=0).astype(BF16)
            o = inter + jnp.dot(a, v_bf, preferred_element_type=F32)
            ms = jnp.mean(o * o, axis=-1, keepdims=True)
            o_ref[rows, cols] = o * lax.rsqrt(ms + NORM_EPS) * nw * gate_ref[rows, cols]
        return carry

    lax.fori_loop(0, HGRN_TM // C, chunk, 0, unroll=HGRN_UNROLL)


def _hgrn(hq, hk, hgl, hi, hgate, norm_w):
    T = hq.shape[0]
    blk = pl.BlockSpec((HGRN_TM, GROUP_W), lambda i: (i, 0))
    return pl.pallas_call(
        _hgrn_kernel,
        grid=(T // HGRN_TM,),
        in_specs=[blk, blk, blk, blk, blk, pl.BlockSpec((1, HEAD_W), lambda i: (0, 0))],
        out_specs=blk,
        out_shape=jax.ShapeDtypeStruct((T, GROUP_W), F32),
        scratch_shapes=[pltpu.VMEM((N_HEADS, HEAD_W, HEAD_W), F32),
                        pltpu.VMEM((HGRN_CHUNK, GROUP_W), F32)],
        compiler_params=pltpu.CompilerParams(
            dimension_semantics=("arbitrary",), vmem_limit_bytes=VMEM_LIMIT),
        name="hgrn2",
    )(hq, hk, hgl, hi, hgate, norm_w)


def _attn_kernel(lq1_ref, lk1_ref, lq2_ref, lk2_ref, q_ref, k_ref, vt_ref, nw_ref,
                 o_ref, s_ref, p_ref, alpha_ref, acc_ref, *, lambda_init):
    n_q = q_ref.shape[0] // ATT_TQ
    n_steps = n_q * (n_q + 1) // 2
    assert n_steps % 2 == 0 and ATT_TQ == ATT_TK
    n_t = ATT_TQ // ATT_TN
    lam = (jnp.exp(jnp.sum(lq1_ref[...] * lk1_ref[...]))
           - jnp.exp(jnp.sum(lq2_ref[...] * lk2_ref[...])) + lambda_init)
    lane = lax.broadcasted_iota(jnp.int32, (ATT_TQ, HEAD_W), 1)
    neg_inf = jnp.full((1, ATT_TN), -jnp.inf, F32)

    acc_ref[...] = jnp.zeros_like(acc_ref)

    def advance(pos):
        qi, t = pos
        row_end = t == qi
        qi_n = jnp.where(row_end, qi + 1, qi)
        t_n = jnp.where(row_end, 0, t + 1)
        done = qi_n == n_q
        return jnp.where(done, n_q - 1, qi_n), jnp.where(done, n_q - 1, t_n)

    def scores(pos, slot, masked):
        qi, t = pos
        q = q_ref[pl.ds(pl.multiple_of(qi * ATT_TQ, ATT_TQ), ATT_TQ), :]
        kb = k_ref[pl.ds(pl.multiple_of(t * ATT_TK, ATT_TK), ATT_TK), :]
        zero = jnp.zeros_like(q)
        q_maps = (jnp.where(lane < QK_DIM, q, zero), jnp.where(lane >= QK_DIM, q, zero))
        mx = []
        for c in range(2):
            for n in range(n_t):
                cols = slice(n * ATT_TN, (n + 1) * ATT_TN)
                s = lax.dot_general(kb, q_maps[c][cols, :], NT,
                                    preferred_element_type=F32)
                if masked:
                    kpos = lax.broadcasted_iota(jnp.int32, s.shape, 0) + t * ATT_TK
                    qpos = lax.broadcasted_iota(jnp.int32, s.shape, 1) + (qi * ATT_TQ + n * ATT_TN)
                    s = jnp.where(kpos <= qpos, s, -jnp.inf)
                s_ref[slot, c, :, cols] = s
                mx.append(jnp.max(s, axis=0, keepdims=True))
        return tuple(mx)

    def softmax(pos, slot, mx, m_run):
        first = pos[1] == 0
        m_out = []
        for c in range(2):
            for n in range(n_t):
                i = c * n_t + n
                cols = slice(n * ATT_TN, (n + 1) * ATT_TN)
                m_old = jnp.where(first, neg_inf, m_run[i])
                m_new = jnp.maximum(m_old, mx[i])
                alpha_ref[slot, c, :, cols] = jnp.exp2(m_old - m_new)
                p_ref[slot, c, :, cols] = jnp.exp2(s_ref[slot, c, :, cols] - m_new).astype(BF16)
                m_out.append(m_new)
        return tuple(m_out)

    def pv(pos, slot):
        qi, t = pos
        vt = vt_ref[:, pl.ds(pl.multiple_of(t * ATT_TK, ATT_TK), ATT_TK)]
        for c in range(2):
            upd = jnp.dot(vt, p_ref[slot, c], preferred_element_type=F32)
            acc_ref[qi % 2, c] = alpha_ref[slot, c] * acc_ref[qi % 2, c] + upd

    def finalize(qi):
        a1 = acc_ref[qi % 2, 0]
        a2 = acc_ref[qi % 2, 1]
        o_t = (a1[:HEAD_W] / a1[HEAD_W:HEAD_W + 1]
               - lam * (a2[:HEAD_W] / a2[HEAD_W:HEAD_W + 1]))
        o = o_t.T
        ms = jnp.mean(o * o, axis=-1, keepdims=True)
        rows = pl.ds(pl.multiple_of(qi * ATT_TQ, ATT_TQ), ATT_TQ)
        o_ref[rows, :] = o * lax.rsqrt(ms + SUBLN_EPS) * nw_ref[...] * (1.0 - lambda_init)

    def on_diag(pos):
        return pos[0] == pos[1]

    def pair_body(pos_a, pos_b, pos_c, mx, m_run, diag_a0, diag_a1):
        mx_mid = scores(pos_a[0], 0, diag_a0)
        m_mid = softmax(pos_b[0], 1, mx, m_run)
        pv(pos_c[0], 0)
        mx_out = scores(pos_a[1], 1, diag_a1)
        m_out = softmax(pos_b[1], 0, mx_mid, m_mid)
        pv(pos_c[1], 1)
        return mx_out, m_out

    variants = [(False, False), (True, False), (False, True)]

    def pair(_, carry):
        pos_a0, pos_b0, pos_c0, mx, m_run = carry
        pos_a, pos_b, pos_c = ((p, advance(p)) for p in (pos_a0, pos_b0, pos_c0))
        idx = on_diag(pos_a[0]).astype(jnp.int32) + 2 * on_diag(pos_a[1]).astype(jnp.int32)
        branches = [functools.partial(lambda flags, _: pair_body(pos_a, pos_b, pos_c, mx, m_run,
                                                                 *flags), flags)
                    for flags in variants]
        mx, m_run = lax.switch(idx, branches, 0)

        @pl.when(on_diag(pos_c[0]) | on_diag(pos_c[1]))
        def _():
            finalize(jnp.where(on_diag(pos_c[0]), pos_c[0][0], pos_c[1][0]))

        return advance(pos_a[1]), advance(pos_b[1]), advance(pos_c[1]), mx, m_run

    pos0 = (jnp.int32(0), jnp.int32(0))
    pos1 = advance(pos0)
    pos2 = advance(pos1)
    mx0 = scores(pos0, 0, True)
    mx1 = scores(pos1, 1, False)
    m_run = softmax(pos0, 0, mx0, (neg_inf,) * (2 * n_t))

    lax.fori_loop(0, n_steps // 2, pair, (pos2, pos1, pos0, mx1, m_run))


def _diff_attn(lq1, lk1, lq2, lk2, dq, dk, dvt, norm_w, lambda_init):
    T = dq.shape[0]
    lam_spec = pl.BlockSpec((1, QK_DIM), lambda h: (0, 0))
    return pl.pallas_call(
        functools.partial(_attn_kernel, lambda_init=lambda_init),
        grid=(N_HEADS,),
        in_specs=[
            lam_spec, lam_spec, lam_spec, lam_spec,
            pl.BlockSpec((T, HEAD_W), lambda h: (0, h)),
            pl.BlockSpec((T, HEAD_W), lambda h: (0, h)),
            pl.BlockSpec((None, VT_ROWS, T), lambda h: (h, 0, 0)),
            pl.BlockSpec((1, HEAD_W), lambda h: (0, 0)),
        ],
        out_specs=pl.BlockSpec((T, HEAD_W), lambda h: (0, h)),
        out_shape=jax.ShapeDtypeStruct((T, GROUP_W), F32),
        scratch_shapes=[
            pltpu.VMEM((2, 2, ATT_TK, ATT_TQ), F32),
            pltpu.VMEM((2, 2, ATT_TK, ATT_TQ), BF16),
            pltpu.VMEM((2, 2, 1, ATT_TQ), F32),
            pltpu.VMEM((2, 2, VT_ROWS, ATT_TQ), F32),
        ],
        compiler_params=pltpu.CompilerParams(
            dimension_semantics=("arbitrary",), vmem_limit_bytes=VMEM_LIMIT),
        name="diff_attn",
    )(lq1, lk1, lq2, lk2, dq, dk, dvt, norm_w)


def _out_proj_kernel(ho_ref, do_ref, dgate_ref, x_ref, w_ref, nw_ref, o_ref):
    y = x_ref[...]
    y = y + jnp.dot(ho_ref[...].astype(BF16), w_ref[:GROUP_W, :], preferred_element_type=F32)
    mix_d = (do_ref[...] * dgate_ref[...]).astype(BF16)
    y = y + jnp.dot(mix_d, w_ref[GROUP_W:, :], preferred_element_type=F32)
    ms = jnp.mean(y * y, axis=-1, keepdims=True)
    o_ref[...] = y * lax.rsqrt(ms + NORM_EPS) * nw_ref[...]


def _out_proj(ho, do, dgate, x2, w_out_bf, final_norm_w):
    T, D = x2.shape
    row = lambda i: (i, 0)
    fix = lambda i: (0, 0)
    return pl.pallas_call(
        _out_proj_kernel,
        grid=(T // PROJ_TM,),
        in_specs=[
            pl.BlockSpec((PROJ_TM, GROUP_W), row),
            pl.BlockSpec((PROJ_TM, GROUP_W), row),
            pl.BlockSpec((PROJ_TM, GROUP_W), row),
            pl.BlockSpec((PROJ_TM, D), row),
            pl.BlockSpec(w_out_bf.shape, fix),
            pl.BlockSpec((1, D), fix),
        ],
        out_specs=pl.BlockSpec((PROJ_TM, D), row),
        out_shape=jax.ShapeDtypeStruct((T, D), F32),
        compiler_params=pltpu.CompilerParams(
            dimension_semantics=("arbitrary",), vmem_limit_bytes=VMEM_LIMIT),
        name="out_proj",
    )(ho, do, dgate, x2, w_out_bf, final_norm_w)


def _rope_tables(T):
    half = QK_DIM // 2
    inv_freq = 1.0 / (ROPE_THETA ** (jnp.arange(half, dtype=F32) / half))
    ang = jnp.arange(T, dtype=F32)[:, None] * inv_freq[None, :]
    cos, sin = jnp.cos(ang), jnp.sin(ang)
    reps = LANES // QK_DIM
    cos_t = jnp.tile(jnp.concatenate([cos, cos], axis=1), (1, reps))
    sin_t = jnp.tile(jnp.concatenate([-sin, sin], axis=1), (1, reps))
    return cos_t, sin_t


def kernel(x, norm_w, w_in, hgrn_lb_logits, hgrn_norm_w, diff_lambda_q1, diff_lambda_k1,
           diff_lambda_q2, diff_lambda_k2, diff_norm_w, w_out, final_norm_w):
    B, T, D = x.shape
    depth = norm_w.shape[0]
    assert B == 1 and depth == 1 and D == 2 * GROUP_W
    assert T % ATT_TQ == 0 and T % HGRN_TM == 0 and T % PROJ_TM == 0
    x2 = x.reshape(T, D)
    cos_t, sin_t = _rope_tables(T)
    l = 0
    lambda_init = 0.8 - 0.6 * math.exp(-0.3 * l)
    hq, hk, hgl, hi, hgate, dq, dk, dvt, dgate = _in_proj(
        x2, norm_w[l:l + 1], w_in[l].astype(BF16), hgrn_lb_logits, cos_t, sin_t)
    ho = _hgrn(hq, hk, hgl, hi, hgate, hgrn_norm_w[l:l + 1])
    do = _diff_attn(diff_lambda_q1[l:l + 1], diff_lambda_k1[l:l + 1],
                    diff_lambda_q2[l:l + 1], diff_lambda_k2[l:l + 1],
                    dq, dk, dvt, diff_norm_w[l:l + 1], lambda_init)
    out = _out_proj(ho, do, dgate, x2, w_out[l].astype(BF16), final_norm_w.reshape(1, D))
    return out.reshape(B, T, D)
```

```python
import functools
import math

import jax
import jax.numpy as jnp
from jax import lax
from jax.experimental import pallas as pl
from jax.experimental.pallas import tpu as pltpu

F32 = jnp.float32
BF16 = jnp.bfloat16

LANES = 128
SUBLANES = 8
VMEM_LIMIT = 56 * 1024 * 1024

N_HEADS = 4
HEAD_W = 128
GROUP_W = N_HEADS * HEAD_W
QK_DIM = 64
ROPE_THETA = 10000.0
NORM_EPS = 1e-6
SUBLN_EPS = 1e-5

PROJ_TM = 512
HGRN_CHUNK = 64
HGRN_SUB = 16
HGRN_TM = 512
HGRN_UNROLL = 4
ATT_TQ = 512
ATT_TK = 512
ATT_TN = 256
VT_ROWS = HEAD_W + 16
LOG2E = math.log2(math.e)

NT = (((1,), (1,)), ((), ()))
TN = (((0,), (0,)), ((), ()))


def _silu(x):
    return x * (1.0 / (1.0 + jnp.exp(-x)))


def _in_proj_kernel(x_ref, nw_ref, w_ref, lbl_ref, cos_ref, sin_ref,
                    hq_ref, hk_ref, hgl_ref, hi_ref, hgate_ref,
                    dq_ref, dk_ref, dvt_ref, dgate_ref):
    x = x_ref[...]
    ms = jnp.mean(x * x, axis=-1, keepdims=True)
    h = (x * lax.rsqrt(ms + NORM_EPS) * nw_ref[...]).astype(BF16)

    def seg(i):
        return jnp.dot(h, w_ref[:, i * GROUP_W:(i + 1) * GROUP_W],
                       preferred_element_type=F32)

    lg = lbl_ref[...]
    e = jnp.exp(lg - jnp.max(lg, axis=0, keepdims=True))
    lb = e[0:1, :] / jnp.sum(e, axis=0, keepdims=True)

    hq_ref[...] = seg(0)
    f = lb + (1.0 - lb) * (1.0 / (1.0 + jnp.exp(-seg(1))))
    hk_ref[...] = 1.0 - f
    hgl_ref[...] = jnp.log2(f)
    hi_ref[...] = seg(2)
    hgate_ref[...] = _silu(seg(3))

    cos = cos_ref[...]
    sin = sin_ref[...]
    lane = lax.broadcasted_iota(jnp.int32, cos.shape, 1)
    first_half = (lane % QK_DIM) < (QK_DIM // 2)

    def rope(t, scale):
        outs = []
        for hh in range(N_HEADS):
            th = t[:, hh * HEAD_W:(hh + 1) * HEAD_W]
            up = pltpu.roll(th, HEAD_W - QK_DIM // 2, 1)
            dn = pltpu.roll(th, QK_DIM // 2, 1)
            partner = jnp.where(first_half, up, dn)
            outs.append(((th * cos + partner * sin) * scale).astype(BF16))
        return jnp.concatenate(outs, axis=1)

    dq_ref[...] = rope(seg(4), QK_DIM ** -0.5 * LOG2E)
    dk_ref[...] = rope(seg(5), 1.0)
    vt = seg(6).T.astype(BF16)
    for hh in range(N_HEADS):
        dvt_ref[hh, :HEAD_W, :] = vt[hh * HEAD_W:(hh + 1) * HEAD_W, :]
        dvt_ref[hh, HEAD_W:, :] = jnp.ones((VT_ROWS - HEAD_W, PROJ_TM), BF16)
    dgate_ref[...] = _silu(seg(7))


def _in_proj(x2, norm_w, w_in_bf, lb_logits, cos_t, sin_t):
    T, D = x2.shape
    n = T // PROJ_TM
    row = lambda i: (i, 0)
    fix = lambda i: (0, 0)
    f32_out = jax.ShapeDtypeStruct((T, GROUP_W), F32)
    bf_out = jax.ShapeDtypeStruct((T, GROUP_W), BF16)
    blk = pl.BlockSpec((PROJ_TM, GROUP_W), row)
    return pl.pallas_call(
        _in_proj_kernel,
        grid=(n,),
        in_specs=[
            pl.BlockSpec((PROJ_TM, D), row),
            pl.BlockSpec((1, D), fix),
            pl.BlockSpec(w_in_bf.shape, fix),
            pl.BlockSpec(lb_logits.shape, fix),
            pl.BlockSpec((PROJ_TM, LANES), row),
            pl.BlockSpec((PROJ_TM, LANES), row),
        ],
        out_specs=[blk, blk, blk, blk, blk, blk, blk,
                   pl.BlockSpec((N_HEADS, VT_ROWS, PROJ_TM), lambda i: (0, 0, i)), blk],
        out_shape=[f32_out, f32_out, f32_out, f32_out, f32_out, bf_out, bf_out,
                   jax.ShapeDtypeStruct((N_HEADS, VT_ROWS, T), BF16), f32_out],
        compiler_params=pltpu.CompilerParams(
            dimension_semantics=("arbitrary",), vmem_limit_bytes=VMEM_LIMIT),
        name="in_proj",
    )(x2, norm_w, w_in_bf, lb_logits, cos_t, sin_t)


def _hgrn_kernel(q_ref, k_ref, g_ref, v_ref, gate_ref, nw_ref, o_ref, st_ref, c_ref):
    C, SB = HGRN_CHUNK, HGRN_SUB

    @pl.when(pl.program_id(0) == 0)
    def _():
        st_ref[...] = jnp.zeros_like(st_ref)

    ti = lax.broadcasted_iota(jnp.int32, (C, C), 0)
    si = lax.broadcasted_iota(jnp.int32, (C, C), 1)
    tri = (si <= ti).astype(F32)
    a_row = lax.broadcasted_iota(jnp.int32, (SUBLANES, C), 0)
    a_lane = lax.broadcasted_iota(jnp.int32, (SUBLANES, C), 1)
    nw = nw_ref[...]

    def chunk(c, carry):
        r0 = pl.multiple_of(c * C, C)
        rows = pl.ds(r0, C)
        b_all = jnp.dot(tri, g_ref[rows, :], preferred_element_type=F32,
                        precision=lax.Precision.HIGHEST)
        c_ref[...] = b_all - jnp.log2(jnp.maximum(k_ref[rows, :], 0.0))
        for hh in range(N_HEADS):
            cols = slice(hh * HEAD_W, (hh + 1) * HEAD_W)
            b = b_all[:, cols]
            q = q_ref[rows, cols]
            k = k_ref[rows, cols]
            v = v_ref[rows, cols]
            v_bf = v.astype(BF16)
            st = st_ref[hh]
            inter = lax.dot_general((q * jnp.exp2(b)).astype(BF16), st.astype(BF16),
                                    NT, preferred_element_type=F32)
            b_last = b[C - 1:C, :]
            k_dec = (k * jnp.exp2(b_last - b)).astype(BF16)
            st_ref[hh] = st * jnp.exp2(b_last) + lax.dot_general(
                v_bf, k_dec, TN, preferred_element_type=F32)

            pieces = []
            for blk in range(C // SB):
                lo = blk * SB
                if blk > 0:
                    ref = b[lo - 1:lo, :]
                    q_t = (q[lo:lo + SB, :] * jnp.exp2(b[lo:lo + SB, :] - ref)).astype(BF16)
                    k_t = (k[:lo, :] * jnp.exp2(ref - b[:lo, :])).astype(BF16)
                    k_t = jnp.concatenate([k_t, jnp.zeros((C - lo, HEAD_W), BF16)], axis=0)
                    a_far = lax.dot_general(q_t, k_t, NT, preferred_element_type=F32)
                    halves = [a_far[:SUBLANES, :], a_far[SUBLANES:, :]]
                else:
                    halves = [jnp.zeros((SUBLANES, C), F32), jnp.zeros((SUBLANES, C), F32)]
                for s in range(SB):
                    c_s = c_ref[pl.ds(lo + s, 1), cols]
                    for half in range(s // SUBLANES, 2):
                        r = lo + half * SUBLANES
                        w = q[r:r + SUBLANES, :] * jnp.exp2(b[r:r + SUBLANES, :] - c_s)
                        col = jnp.sum(w, axis=-1, keepdims=True)
                        halves[half] = jnp.where(a_lane == lo + s, col, halves[half])
                for half in range(2):
                    keep = a_lane <= a_row + (lo + half * SUBLANES)
                    pieces.append(jnp.where(keep, halves[half], 0.0))
            a = jnp.concatenate(pieces, axis=0).astype(BF16)
            o = inter + jnp.dot(a, v_bf, preferred_element_type=F32)
            ms = jnp.mean(o * o, axis=-1, keepdims=True)
            o_ref[rows, cols] = o * lax.rsqrt(ms + NORM_EPS) * nw * gate_ref[rows, cols]
        return carry

    lax.fori_loop(0, HGRN_TM // C, chunk, 0, unroll=HGRN_UNROLL)


def _hgrn(hq, hk, hgl, hi, hgate, norm_w):
    T = hq.shape[0]
    blk = pl.BlockSpec((HGRN_TM, GROUP_W), lambda i: (i, 0))
    return pl.pallas_call(
        _hgrn_kernel,
        grid=(T // HGRN_TM,),
        in_specs=[blk, blk, blk, blk, blk, pl.BlockSpec((1, HEAD_W), lambda i: (0, 0))],
        out_specs=blk,
        out_shape=jax.ShapeDtypeStruct((T, GROUP_W), F32),
        scratch_shapes=[pltpu.VMEM((N_HEADS, HEAD_W, HEAD_W), F32),
                        pltpu.VMEM((HGRN_CHUNK, GROUP_W), F32)],
        compiler_params=pltpu.CompilerParams(
            dimension_semantics=("arbitrary",), vmem_limit_bytes=VMEM_LIMIT),
        name="hgrn2",
    )(hq, hk, hgl, hi, hgate, norm_w)


def _attn_kernel(lq1_ref, lk1_ref, lq2_ref, lk2_ref, q_ref, k_ref, vt_ref, nw_ref,
                 o_ref, s_ref, acc_ref, *, lambda_init):
    n_q = q_ref.shape[0] // ATT_TQ
    n_steps = n_q * (n_q + 1) // 2
    assert n_steps % 2 == 0 and ATT_TQ == ATT_TK
    n_t = ATT_TQ // ATT_TN
    lam = (jnp.exp(jnp.sum(lq1_ref[...] * lk1_ref[...]))
           - jnp.exp(jnp.sum(lq2_ref[...] * lk2_ref[...])) + lambda_init)
    lane = lax.broadcasted_iota(jnp.int32, (ATT_TQ, HEAD_W), 1)
    neg_inf = jnp.full((1, ATT_TN), -jnp.inf, F32)

    acc_ref[...] = jnp.zeros_like(acc_ref)

    def advance(pos):
        qi, t = pos
        row_end = t == qi
        qi_n = jnp.where(row_end, qi + 1, qi)
        t_n = jnp.where(row_end, 0, t + 1)
        done = qi_n == n_q
        return jnp.where(done, n_q - 1, qi_n), jnp.where(done, n_q - 1, t_n)

    def scores(pos, slot, masked):
        qi, t = pos
        q = q_ref[pl.ds(pl.multiple_of(qi * ATT_TQ, ATT_TQ), ATT_TQ), :]
        kb = k_ref[pl.ds(pl.multiple_of(t * ATT_TK, ATT_TK), ATT_TK), :]
        zero = jnp.zeros_like(q)
        q_maps = (jnp.where(lane < QK_DIM, q, zero), jnp.where(lane >= QK_DIM, q, zero))
        mx = []
        for c in range(2):
            for n in range(n_t):
                cols = slice(n * ATT_TN, (n + 1) * ATT_TN)
                s = lax.dot_general(kb, q_maps[c][cols, :], NT,
                                    preferred_element_type=F32)
                if masked:
                    kpos = lax.broadcasted_iota(jnp.int32, s.shape, 0) + t * ATT_TK
                    qpos = lax.broadcasted_iota(jnp.int32, s.shape, 1) + (qi * ATT_TQ + n * ATT_TN)
                    s = jnp.where(kpos <= qpos, s, -jnp.inf)
                s_ref[slot, c, :, cols] = s
                mx.append(jnp.max(s, axis=0, keepdims=True))
        return tuple(mx)

    def softmax_pv(pos, slot, mx, m_run):
        qi, t = pos
        first = t == 0
        vt = vt_ref[:, pl.ds(pl.multiple_of(t * ATT_TK, ATT_TK), ATT_TK)]
        m_out = []
        for c in range(2):
            for n in range(n_t):
                i = c * n_t + n
                cols = slice(n * ATT_TN, (n + 1) * ATT_TN)
                m_old = jnp.where(first, neg_inf, m_run[i])
                m_new = jnp.maximum(m_old, mx[i])
                alpha = jnp.exp2(m_old - m_new)
                p = jnp.exp2(s_ref[slot, c, :, cols] - m_new).astype(BF16)
                upd = jnp.dot(vt, p, preferred_element_type=F32)
                acc_ref[qi % 2, c, :, cols] = alpha * acc_ref[qi % 2, c, :, cols] + upd
                m_out.append(m_new)
        return tuple(m_out)

    def finalize(qi):
        a1 = acc_ref[qi % 2, 0]
        a2 = acc_ref[qi % 2, 1]
        o_t = (a1[:HEAD_W] / a1[HEAD_W:HEAD_W + 1]
               - lam * (a2[:HEAD_W] / a2[HEAD_W:HEAD_W + 1]))
        o = o_t.T
        ms = jnp.mean(o * o, axis=-1, keepdims=True)
        rows = pl.ds(pl.multiple_of(qi * ATT_TQ, ATT_TQ), ATT_TQ)
        o_ref[rows, :] = o * lax.rsqrt(ms + SUBLN_EPS) * nw_ref[...] * (1.0 - lambda_init)

    def on_diag(pos):
        return pos[0] == pos[1]

    def pair_body(pos_a, pos_b, mx, m_run, diag_a0, diag_a1):
        mx_mid = scores(pos_a[0], 1, diag_a0)
        m_mid = softmax_pv(pos_b[0], 0, mx, m_run)
        mx_out = scores(pos_a[1], 0, diag_a1)
        m_out = softmax_pv(pos_b[1], 1, mx_mid, m_mid)
        return mx_out, m_out

    variants = [(False, False), (True, False), (False, True)]
    last_pair = n_steps // 2 - 1

    def pair(i, carry):
        pos_a0, pos_b0, mx, m_run = carry
        pos_a, pos_b = ((p, advance(p)) for p in (pos_a0, pos_b0))
        diag_a1 = on_diag(pos_a[1]) & (i < last_pair)
        idx = on_diag(pos_a[0]).astype(jnp.int32) + 2 * diag_a1.astype(jnp.int32)
        branches = [functools.partial(lambda flags, _: pair_body(pos_a, pos_b, mx, m_run, *flags),
                                      flags)
                    for flags in variants]
        mx, m_run = lax.switch(idx, branches, 0)

        @pl.when(on_diag(pos_b[0]) | on_diag(pos_b[1]))
        def _():
            finalize(jnp.where(on_diag(pos_b[0]), pos_b[0][0], pos_b[1][0]))

        return advance(pos_a[1]), advance(pos_b[1]), mx, m_run

    pos0 = (jnp.int32(0), jnp.int32(0))
    mx0 = scores(pos0, 0, True)
    m0 = (neg_inf,) * (2 * n_t)
    lax.fori_loop(0, n_steps // 2, pair, (advance(pos0), pos0, mx0, m0))


def _diff_attn(lq1, lk1, lq2, lk2, dq, dk, dvt, norm_w, lambda_init):
    T = dq.shape[0]
    lam_spec = pl.BlockSpec((1, QK_DIM), lambda h: (0, 0))
    return pl.pallas_call(
        functools.partial(_attn_kernel, lambda_init=lambda_init),
        grid=(N_HEADS,),
        in_specs=[
            lam_spec, lam_spec, lam_spec, lam_spec,
            pl.BlockSpec((T, HEAD_W), lambda h: (0, h)),
            pl.BlockSpec((T, HEAD_W), lambda h: (0, h)),
            pl.BlockSpec((None, VT_ROWS, T), lambda h: (h, 0, 0)),
            pl.BlockSpec((1, HEAD_W), lambda h: (0, 0)),
        ],
        out_specs=pl.BlockSpec((T, HEAD_W), lambda h: (0, h)),
        out_shape=jax.ShapeDtypeStruct((T, GROUP_W), F32),
        scratch_shapes=[
            pltpu.VMEM((2, 2, ATT_TK, ATT_TQ), F32),
            pltpu.VMEM((2, 2, VT_ROWS, ATT_TQ), F32),
        ],
        compiler_params=pltpu.CompilerParams(
            dimension_semantics=("arbitrary",), vmem_limit_bytes=VMEM_LIMIT),
        name="diff_attn",
    )(lq1, lk1, lq2, lk2, dq, dk, dvt, norm_w)


def _out_proj_kernel(ho_ref, do_ref, dgate_ref, x_ref, w_ref, nw_ref, o_ref):
    y = x_ref[...]
    y = y + jnp.dot(ho_ref[...].astype(BF16), w_ref[:GROUP_W, :], preferred_element_type=F32)
    mix_d = (do_ref[...] * dgate_ref[...]).astype(BF16)
    y = y + jnp.dot(mix_d, w_ref[GROUP_W:, :], preferred_element_type=F32)
    ms = jnp.mean(y * y, axis=-1, keepdims=True)
    o_ref[...] = y * lax.rsqrt(ms + NORM_EPS) * nw_ref[...]


def _out_proj(ho, do, dgate, x2, w_out_bf, final_norm_w):
    T, D = x2.shape
    row = lambda i: (i, 0)
    fix = lambda i: (0, 0)
    return pl.pallas_call(
        _out_proj_kernel,
        grid=(T // PROJ_TM,),
        in_specs=[
            pl.BlockSpec((PROJ_TM, GROUP_W), row),
            pl.BlockSpec((PROJ_TM, GROUP_W), row),
            pl.BlockSpec((PROJ_TM, GROUP_W), row),
            pl.BlockSpec((PROJ_TM, D), row),
            pl.BlockSpec(w_out_bf.shape, fix),
            pl.BlockSpec((1, D), fix),
        ],
        out_specs=pl.BlockSpec((PROJ_TM, D), row),
        out_shape=jax.ShapeDtypeStruct((T, D), F32),
        compiler_params=pltpu.CompilerParams(
            dimension_semantics=("arbitrary",), vmem_limit_bytes=VMEM_LIMIT),
        name="out_proj",
    )(ho, do, dgate, x2, w_out_bf, final_norm_w)


def _rope_tables(T):
    half = QK_DIM // 2
    inv_freq = 1.0 / (ROPE_THETA ** (jnp.arange(half, dtype=F32) / half))
    ang = jnp.arange(T, dtype=F32)[:, None] * inv_freq[None, :]
    cos, sin = jnp.cos(ang), jnp.sin(ang)
    reps = LANES // QK_DIM
    cos_t = jnp.tile(jnp.concatenate([cos, cos], axis=1), (1, reps))
    sin_t = jnp.tile(jnp.concatenate([-sin, sin], axis=1), (1, reps))
    return cos_t, sin_t


def kernel(x, norm_w, w_in, hgrn_lb_logits, hgrn_norm_w, diff_lambda_q1, diff_lambda_k1,
           diff_lambda_q2, diff_lambda_k2, diff_norm_w, w_out, final_norm_w):
    B, T, D = x.shape
    depth = norm_w.shape[0]
    assert B == 1 and depth == 1 and D == 2 * GROUP_W
    assert T % ATT_TQ == 0 and T % HGRN_TM == 0 and T % PROJ_TM == 0
    x2 = x.reshape(T, D)
    cos_t, sin_t = _rope_tables(T)
    l = 0
    lambda_init = 0.8 - 0.6 * math.exp(-0.3 * l)
    hq, hk, hgl, hi, hgate, dq, dk, dvt, dgate = _in_proj(
        x2, norm_w[l:l + 1], w_in[l].astype(BF16), hgrn_lb_logits, cos_t, sin_t)
    ho = _hgrn(hq, hk, hgl, hi, hgate, hgrn_norm_w[l:l + 1])
    do = _diff_attn(diff_lambda_q1[l:l + 1], diff_lambda_k1[l:l + 1],
                    diff_lambda_q2[l:l + 1], diff_lambda_k2[l:l + 1],
                    dq, dk, dvt, diff_norm_w[l:l + 1], lambda_init)
    out = _out_proj(ho, do, dgate, x2, w_out[l].astype(BF16), final_norm_w.reshape(1, D))
    return out.reshape(B, T, D)
```

```python
import functools
import math

import jax
import jax.numpy as jnp
from jax import lax
from jax.experimental import pallas as pl
from jax.experimental.pallas import tpu as pltpu

F32 = jnp.float32
BF16 = jnp.bfloat16

LANES = 128
SUBLANES = 8
VMEM_LIMIT = 56 * 1024 * 1024

N_HEADS = 4
HEAD_W = 128
GROUP_W = N_HEADS * HEAD_W
QK_DIM = 64
ROPE_THETA = 10000.0
NORM_EPS = 1e-6
SUBLN_EPS = 1e-5

PROJ_TM = 512
HGRN_CHUNK = 64
HGRN_SUB = 16
HGRN_TM = 512
HGRN_UNROLL = 8
ATT_TQ = 512
ATT_TK = 512
ATT_TN = 256
VT_ROWS = HEAD_W + 16
LOG2E = math.log2(math.e)

NT = (((1,), (1,)), ((), ()))
TN = (((0,), (0,)), ((), ()))


def _silu(x):
    return x * (1.0 / (1.0 + jnp.exp(-x)))


def _in_proj_kernel(x_ref, nw_ref, w_ref, lbl_ref, cos_ref, sin_ref,
                    hq_ref, hk_ref, hgl_ref, hi_ref, hgate_ref,
                    dq_ref, dk_ref, dvt_ref, dgate_ref):
    x = x_ref[...]
    ms = jnp.mean(x * x, axis=-1, keepdims=True)
    h = (x * lax.rsqrt(ms + NORM_EPS) * nw_ref[...]).astype(BF16)

    def seg(i):
        return jnp.dot(h, w_ref[:, i * GROUP_W:(i + 1) * GROUP_W],
                       preferred_element_type=F32)

    lg = lbl_ref[...]
    e = jnp.exp(lg - jnp.max(lg, axis=0, keepdims=True))
    lb = e[0:1, :] / jnp.sum(e, axis=0, keepdims=True)

    hq_ref[...] = seg(0)
    f = lb + (1.0 - lb) * (1.0 / (1.0 + jnp.exp(-seg(1))))
    hk_ref[...] = 1.0 - f
    hgl_ref[...] = jnp.log2(f)
    hi_ref[...] = seg(2)
    hgate_ref[...] = _silu(seg(3))

    cos = cos_ref[...]
    sin = sin_ref[...]
    lane = lax.broadcasted_iota(jnp.int32, cos.shape, 1)
    first_half = (lane % QK_DIM) < (QK_DIM // 2)

    def rope(t, scale):
        outs = []
        for hh in range(N_HEADS):
            th = t[:, hh * HEAD_W:(hh + 1) * HEAD_W]
            up = pltpu.roll(th, HEAD_W - QK_DIM // 2, 1)
            dn = pltpu.roll(th, QK_DIM // 2, 1)
            partner = jnp.where(first_half, up, dn)
            outs.append(((th * cos + partner * sin) * scale).astype(BF16))
        return jnp.concatenate(outs, axis=1)

    dq_ref[...] = rope(seg(4), QK_DIM ** -0.5 * LOG2E)
    dk_ref[...] = rope(seg(5), 1.0)
    vt = seg(6).T.astype(BF16)
    for hh in range(N_HEADS):
        dvt_ref[hh, :HEAD_W, :] = vt[hh * HEAD_W:(hh + 1) * HEAD_W, :]
        dvt_ref[hh, HEAD_W:, :] = jnp.ones((VT_ROWS - HEAD_W, PROJ_TM), BF16)
    dgate_ref[...] = _silu(seg(7))


def _in_proj(x2, norm_w, w_in_bf, lb_logits, cos_t, sin_t):
    T, D = x2.shape
    n = T // PROJ_TM
    row = lambda i: (i, 0)
    fix = lambda i: (0, 0)
    f32_out = jax.ShapeDtypeStruct((T, GROUP_W), F32)
    bf_out = jax.ShapeDtypeStruct((T, GROUP_W), BF16)
    blk = pl.BlockSpec((PROJ_TM, GROUP_W), row)
    return pl.pallas_call(
        _in_proj_kernel,
        grid=(n,),
        in_specs=[
            pl.BlockSpec((PROJ_TM, D), row),
            pl.BlockSpec((1, D), fix),
            pl.BlockSpec(w_in_bf.shape, fix),
            pl.BlockSpec(lb_logits.shape, fix),
            pl.BlockSpec((PROJ_TM, LANES), row),
            pl.BlockSpec((PROJ_TM, LANES), row),
        ],
        out_specs=[blk, blk, blk, blk, blk, blk, blk,
                   pl.BlockSpec((N_HEADS, VT_ROWS, PROJ_TM), lambda i: (0, 0, i)), blk],
        out_shape=[f32_out, f32_out, f32_out, f32_out, f32_out, bf_out, bf_out,
                   jax.ShapeDtypeStruct((N_HEADS, VT_ROWS, T), BF16), f32_out],
        compiler_params=pltpu.CompilerParams(
            dimension_semantics=("arbitrary",), vmem_limit_bytes=VMEM_LIMIT),
        name="in_proj",
    )(x2, norm_w, w_in_bf, lb_logits, cos_t, sin_t)


def _hgrn_kernel(q_ref, k_ref, g_ref, v_ref, gate_ref, nw_ref, o_ref, st_ref, c_ref):
    C, SB = HGRN_CHUNK, HGRN_SUB

    @pl.when(pl.program_id(0) == 0)
    def _():
        st_ref[...] = jnp.zeros_like(st_ref)

    ti = lax.broadcasted_iota(jnp.int32, (C, C), 0)
    si = lax.broadcasted_iota(jnp.int32, (C, C), 1)
    tri = (si <= ti).astype(F32)
    a_row = lax.broadcasted_iota(jnp.int32, (SUBLANES, C), 0)
    a_lane = lax.broadcasted_iota(jnp.int32, (SUBLANES, C), 1)
    nw = nw_ref[...]

    def chunk(c, carry):
        r0 = pl.multiple_of(c * C, C)
        rows = pl.ds(r0, C)
        b_all = jnp.dot(tri, g_ref[rows, :], preferred_element_type=F32,
                        precision=lax.Precision.HIGHEST)
        c_ref[...] = b_all - jnp.log2(jnp.maximum(k_ref[rows, :], 0.0))
        for hh in range(N_HEADS):
            cols = slice(hh * HEAD_W, (hh + 1) * HEAD_W)
            b = b_all[:, cols]
            q = q_ref[rows, cols]
            k = k_ref[rows, cols]
            v = v_ref[rows, cols]
            v_bf = v.astype(BF16)
            st = st_ref[hh]
            inter = lax.dot_general((q * jnp.exp2(b)).astype(BF16), st.astype(BF16),
                                    NT, preferred_element_type=F32)
            b_last = b[C - 1:C, :]
            k_dec = (k * jnp.exp2(b_last - b)).astype(BF16)
            st_ref[hh] = st * jnp.exp2(b_last) + lax.dot_general(
                v_bf, k_dec, TN, preferred_element_type=F32)

            q_near, k_near = [], []
            for blk in range(C // SB):
                lo, mid = blk * SB, blk * SB + SUBLANES
                ref = b[mid - 1:mid, :]
                q_near.append(q[mid:mid + SUBLANES, :] * jnp.exp2(b[mid:mid + SUBLANES, :] - ref))
                k_near.append(k[lo:mid, :] * jnp.exp2(ref - b[lo:mid, :]))
                k_near.append(jnp.zeros((SUBLANES, HEAD_W), F32))
            a_near = lax.dot_general(jnp.concatenate(q_near, axis=0).astype(BF16),
                                     jnp.concatenate(k_near, axis=0).astype(BF16),
                                     NT, preferred_element_type=F32)
            pieces = []
            for blk in range(C // SB):
                lo = blk * SB
                near = a_near[blk * SUBLANES:(blk + 1) * SUBLANES, :]
                if blk > 0:
                    ref = b[lo - 1:lo, :]
                    q_t = (q[lo:lo + SB, :] * jnp.exp2(b[lo:lo + SB, :] - ref)).astype(BF16)
                    k_t = (k[:lo, :] * jnp.exp2(ref - b[:lo, :])).astype(BF16)
                    k_t = jnp.concatenate([k_t, jnp.zeros((C - lo, HEAD_W), BF16)], axis=0)
                    a_far = lax.dot_general(q_t, k_t, NT, preferred_element_type=F32)
                    halves = [a_far[:SUBLANES, :],
                              jnp.where(a_lane >= lo, near, a_far[SUBLANES:, :])]
                else:
                    halves = [jnp.zeros((SUBLANES, C), F32), near]
                for s in range(SB):
                    half = s // SUBLANES
                    r = lo + half * SUBLANES
                    c_s = c_ref[pl.ds(lo + s, 1), cols]
                    w = q[r:r + SUBLANES, :] * jnp.exp2(b[r:r + SUBLANES, :] - c_s)
                    col = jnp.sum(w, axis=-1, keepdims=True)
                    halves[half] = jnp.where(a_lane == lo + s, col, halves[half])
                for half in range(2):
                    keep = a_lane <= a_row + (lo + half * SUBLANES)
                    pieces.append(jnp.where(keep, halves[half], 0.0))
            a = jnp.concatenate(pieces, axis=0).astype(BF16)
            o = inter + jnp.dot(a, v_bf, preferred_element_type=F32)
            ms = jnp.mean(o * o, axis=-1, keepdims=True)
            o_ref[rows, cols] = (o * lax.rsqrt(ms + NORM_EPS) * nw
                                 * gate_ref[rows, cols]).astype(BF16)
        return carry

    lax.fori_loop(0, HGRN_TM // C, chunk, 0, unroll=HGRN_UNROLL)


def _hgrn(hq, hk, hgl, hi, hgate, norm_w):
    T = hq.shape[0]
    blk = pl.BlockSpec((HGRN_TM, GROUP_W), lambda i: (i, 0))
    return pl.pallas_call(
        _hgrn_kernel,
        grid=(T // HGRN_TM,),
        in_specs=[blk, blk, blk, blk, blk, pl.BlockSpec((1, HEAD_W), lambda i: (0, 0))],
        out_specs=blk,
        out_shape=jax.ShapeDtypeStruct((T, GROUP_W), BF16),
        scratch_shapes=[pltpu.VMEM((N_HEADS, HEAD_W, HEAD_W), F32),
                        pltpu.VMEM((HGRN_CHUNK, GROUP_W), F32)],
        compiler_params=pltpu.CompilerParams(
            dimension_semantics=("arbitrary",), vmem_limit_bytes=VMEM_LIMIT),
        name="hgrn2",
    )(hq, hk, hgl, hi, hgate, norm_w)


def _attn_kernel(lq1_ref, lk1_ref, lq2_ref, lk2_ref, q_ref, k_ref, vt_ref, nw_ref,
                 o_ref, s_ref, acc_ref, *, lambda_init):
    n_q = q_ref.shape[0] // ATT_TQ
    n_steps = n_q * (n_q + 1) // 2
    assert n_steps % 2 == 0 and ATT_TQ == ATT_TK
    n_t = ATT_TQ // ATT_TN
    lam = (jnp.exp(jnp.sum(lq1_ref[...] * lk1_ref[...]))
           - jnp.exp(jnp.sum(lq2_ref[...] * lk2_ref[...])) + lambda_init)
    lane = lax.broadcasted_iota(jnp.int32, (ATT_TQ, HEAD_W), 1)
    neg_inf = jnp.full((1, ATT_TN), -jnp.inf, F32)

    acc_ref[...] = jnp.zeros_like(acc_ref)

    def advance(pos):
        qi, t = pos
        row_end = t == qi
        qi_n = jnp.where(row_end, qi + 1, qi)
        t_n = jnp.where(row_end, 0, t + 1)
        done = qi_n == n_q
        return jnp.where(done, n_q - 1, qi_n), jnp.where(done, n_q - 1, t_n)

    def scores(pos, slot, masked):
        qi, t = pos
        q = q_ref[pl.ds(pl.multiple_of(qi * ATT_TQ, ATT_TQ), ATT_TQ), :]
        kb = k_ref[pl.ds(pl.multiple_of(t * ATT_TK, ATT_TK), ATT_TK), :]
        zero = jnp.zeros_like(q)
        q_maps = (jnp.where(lane < QK_DIM, q, zero), jnp.where(lane >= QK_DIM, q, zero))
        mx = []
        for c in range(2):
            for n in range(n_t):
                cols = slice(n * ATT_TN, (n + 1) * ATT_TN)
                s = lax.dot_general(kb, q_maps[c][cols, :], NT,
                                    preferred_element_type=F32)
                if masked:
                    kpos = lax.broadcasted_iota(jnp.int32, s.shape, 0) + t * ATT_TK
                    qpos = lax.broadcasted_iota(jnp.int32, s.shape, 1) + (qi * ATT_TQ + n * ATT_TN)
                    s = jnp.where(kpos <= qpos, s, -jnp.inf)
                s_ref[slot, c, :, cols] = s
                mx.append(jnp.max(s, axis=0, keepdims=True))
        return tuple(mx)

    def softmax_pv(pos, slot, mx, m_run):
        qi, t = pos
        first = t == 0
        vt = vt_ref[:, pl.ds(pl.multiple_of(t * ATT_TK, ATT_TK), ATT_TK)]
        m_out = []
        for c in range(2):
            for n in range(n_t):
                i = c * n_t + n
                cols = slice(n * ATT_TN, (n + 1) * ATT_TN)
                m_old = jnp.where(first, neg_inf, m_run[i])
                m_new = jnp.maximum(m_old, mx[i])
                alpha = jnp.exp2(m_old - m_new)
                p = jnp.exp2(s_ref[slot, c, :, cols] - m_new).astype(BF16)
                upd = jnp.dot(vt, p, preferred_element_type=F32)
                acc_ref[qi % 2, c, :, cols] = alpha * acc_ref[qi % 2, c, :, cols] + upd
                m_out.append(m_new)
        return tuple(m_out)

    def finalize(qi):
        a1 = acc_ref[qi % 2, 0]
        a2 = acc_ref[qi % 2, 1]
        o_t = (a1[:HEAD_W] / a1[HEAD_W:HEAD_W + 1]
               - lam * (a2[:HEAD_W] / a2[HEAD_W:HEAD_W + 1]))
        o = o_t.T
        ms = jnp.mean(o * o, axis=-1, keepdims=True)
        rows = pl.ds(pl.multiple_of(qi * ATT_TQ, ATT_TQ), ATT_TQ)
        o_ref[rows, :] = o * lax.rsqrt(ms + SUBLN_EPS) * nw_ref[...] * (1.0 - lambda_init)

    def on_diag(pos):
        return pos[0] == pos[1]

    def pair_body(pos_a, pos_b, mx, m_run, diag_a0, diag_a1):
        mx_mid = scores(pos_a[0], 1, diag_a0)
        m_mid = softmax_pv(pos_b[0], 0, mx, m_run)
        mx_out = scores(pos_a[1], 0, diag_a1)
        m_out = softmax_pv(pos_b[1], 1, mx_mid, m_mid)
        return mx_out, m_out

    variants = [(False, False), (True, False), (False, True)]
    last_pair = n_steps // 2 - 1

    def pair(i, carry):
        pos_a0, pos_b0, mx, m_run = carry
        pos_a, pos_b = ((p, advance(p)) for p in (pos_a0, pos_b0))
        diag_a1 = on_diag(pos_a[1]) & (i < last_pair)
        idx = on_diag(pos_a[0]).astype(jnp.int32) + 2 * diag_a1.astype(jnp.int32)
        branches = [functools.partial(lambda flags, _: pair_body(pos_a, pos_b, mx, m_run, *flags),
                                      flags)
                    for flags in variants]
        mx, m_run = lax.switch(idx, branches, 0)

        @pl.when(on_diag(pos_b[0]) | on_diag(pos_b[1]))
        def _():
            finalize(jnp.where(on_diag(pos_b[0]), pos_b[0][0], pos_b[1][0]))

        return advance(pos_a[1]), advance(pos_b[1]), mx, m_run

    pos0 = (jnp.int32(0), jnp.int32(0))
    mx0 = scores(pos0, 0, True)
    m0 = (neg_inf,) * (2 * n_t)
    lax.fori_loop(0, n_steps // 2, pair, (advance(pos0), pos0, mx0, m0))


def _diff_attn(lq1, lk1, lq2, lk2, dq, dk, dvt, norm_w, lambda_init):
    T = dq.shape[0]
    lam_spec = pl.BlockSpec((1, QK_DIM), lambda h: (0, 0))
    return pl.pallas_call(
        functools.partial(_attn_kernel, lambda_init=lambda_init),
        grid=(N_HEADS,),
        in_specs=[
            lam_spec, lam_spec, lam_spec, lam_spec,
            pl.BlockSpec((T, HEAD_W), lambda h: (0, h)),
            pl.BlockSpec((T, HEAD_W), lambda h: (0, h)),
            pl.BlockSpec((None, VT_ROWS, T), lambda h: (h, 0, 0)),
            pl.BlockSpec((1, HEAD_W), lambda h: (0, 0)),
        ],
        out_specs=pl.BlockSpec((T, HEAD_W), lambda h: (0, h)),
        out_shape=jax.ShapeDtypeStruct((T, GROUP_W), F32),
        scratch_shapes=[
            pltpu.VMEM((2, 2, ATT_TK, ATT_TQ), F32),
            pltpu.VMEM((2, 2, VT_ROWS, ATT_TQ), F32),
        ],
        compiler_params=pltpu.CompilerParams(
            dimension_semantics=("arbitrary",), vmem_limit_bytes=VMEM_LIMIT),
        name="diff_attn",
    )(lq1, lk1, lq2, lk2, dq, dk, dvt, norm_w)


def _out_proj_kernel(ho_ref, do_ref, dgate_ref, x_ref, w_ref, nw_ref, o_ref):
    y = x_ref[...]
    y = y + jnp.dot(ho_ref[...], w_ref[:GROUP_W, :], preferred_element_type=F32)
    mix_d = (do_ref[...] * dgate_ref[...]).astype(BF16)
    y = y + jnp.dot(mix_d, w_ref[GROUP_W:, :], preferred_element_type=F32)
    ms = jnp.mean(y * y, axis=-1, keepdims=True)
    o_ref[...] = y * lax.rsqrt(ms + NORM_EPS) * nw_ref[...]


def _out_proj(ho, do, dgate, x2, w_out_bf, final_norm_w):
    T, D = x2.shape
    row = lambda i: (i, 0)
    fix = lambda i: (0, 0)
    return pl.pallas_call(
        _out_proj_kernel,
        grid=(T // PROJ_TM,),
        in_specs=[
            pl.BlockSpec((PROJ_TM, GROUP_W), row),
            pl.BlockSpec((PROJ_TM, GROUP_W), row),
            pl.BlockSpec((PROJ_TM, GROUP_W), row),
            pl.BlockSpec((PROJ_TM, D), row),
            pl.BlockSpec(w_out_bf.shape, fix),
            pl.BlockSpec((1, D), fix),
        ],
        out_specs=pl.BlockSpec((PROJ_TM, D), row),
        out_shape=jax.ShapeDtypeStruct((T, D), F32),
        compiler_params=pltpu.CompilerParams(
            dimension_semantics=("arbitrary",), vmem_limit_bytes=VMEM_LIMIT),
        name="out_proj",
    )(ho, do, dgate, x2, w_out_bf, final_norm_w)


def _rope_tables(T):
    half = QK_DIM // 2
    inv_freq = 1.0 / (ROPE_THETA ** (jnp.arange(half, dtype=F32) / half))
    lane = jnp.arange(LANES)
    ang = jnp.arange(T, dtype=F32)[:, None] * inv_freq[lane % half][None, :]
    sign = jnp.where(lane % QK_DIM < half, -1.0, 1.0).astype(F32)
    return jnp.cos(ang), jnp.sin(ang) * sign[None, :]


def kernel(x, norm_w, w_in, hgrn_lb_logits, hgrn_norm_w, diff_lambda_q1, diff_lambda_k1,
           diff_lambda_q2, diff_lambda_k2, diff_norm_w, w_out, final_norm_w):
    B, T, D = x.shape
    depth = norm_w.shape[0]
    assert B == 1 and depth == 1 and D == 2 * GROUP_W
    assert T % ATT_TQ == 0 and T % HGRN_TM == 0 and T % PROJ_TM == 0
    x2 = x.reshape(T, D)
    cos_t, sin_t = _rope_tables(T)
    l = 0
    lambda_init = 0.8 - 0.6 * math.exp(-0.3 * l)
    hq, hk, hgl, hi, hgate, dq, dk, dvt, dgate = _in_proj(
        x2, norm_w[l:l + 1], w_in[l].astype(BF16), hgrn_lb_logits, cos_t, sin_t)
    ho = _hgrn(hq, hk, hgl, hi, hgate, hgrn_norm_w[l:l + 1])
    do = _diff_attn(diff_lambda_q1[l:l + 1], diff_lambda_k1[l:l + 1],
                    diff_lambda_q2[l:l + 1], diff_lambda_k2[l:l + 1],
                    dq, dk, dvt, diff_norm_w[l:l + 1], lambda_init)
    out = _out_proj(ho, do, dgate, x2, w_out[l].astype(BF16), final_norm_w.reshape(1, D))
    return out.reshape(B, T, D)
```

```python
import functools
import math

import jax
import jax.numpy as jnp
from jax import lax
from jax.experimental import pallas as pl
from jax.experimental.pallas import tpu as pltpu

F32 = jnp.float32
BF16 = jnp.bfloat16

SUBLANES = 8
VMEM_LIMIT = 56 * 1024 * 1024

N_HEADS = 4
HEAD_W = 128
GROUP_W = N_HEADS * HEAD_W
QK_DIM = 64
ROPE_THETA = 10000.0
NORM_EPS = 1e-6
SUBLN_EPS = 1e-5

PROJ_TM = 512
HGRN_CHUNK = 64
HGRN_SUB = 16
HGRN_TM = 512
HGRN_UNROLL = 8
ATT_TQ = 512
ATT_TK = 512
ATT_TN = 256
ATT_STEPS = 4
VT_ROWS = HEAD_W + 16
LOG2E = math.log2(math.e)

NT = (((1,), (1,)), ((), ()))
TN = (((0,), (0,)), ((), ()))


def _silu(x):
    return x * (1.0 / (1.0 + jnp.exp(-x)))


def _in_proj_kernel(x_ref, nw_ref, w_ref, lbl_ref, cos_ref, sin_ref,
                    hq_ref, hk_ref, hgl_ref, hi_ref, hgate_ref,
                    dq_ref, dk_ref, dvt_ref, dgate_ref):
    x = x_ref[...]
    ms = jnp.mean(x * x, axis=-1, keepdims=True)
    h = (x * lax.rsqrt(ms + NORM_EPS) * nw_ref[...]).astype(BF16)

    def seg(i):
        return jnp.dot(h, w_ref[:, i * GROUP_W:(i + 1) * GROUP_W],
                       preferred_element_type=F32)

    lg = lbl_ref[...]
    e = jnp.exp(lg - jnp.max(lg, axis=0, keepdims=True))
    lb = e[0:1, :] / jnp.sum(e, axis=0, keepdims=True)

    hq_ref[...] = seg(0)
    f = lb + (1.0 - lb) * (1.0 / (1.0 + jnp.exp(-seg(1))))
    hk_ref[...] = 1.0 - f
    hgl_ref[...] = jnp.log2(f)
    hi_ref[...] = seg(2)
    hgate_ref[...] = _silu(seg(3))

    reps = HEAD_W // (QK_DIM // 2)
    cos = jnp.concatenate([cos_ref[...]] * reps, axis=0).T
    sin_half = sin_ref[...]
    sin = jnp.concatenate([-sin_half, sin_half] * (reps // 2), axis=0).T
    lane = lax.broadcasted_iota(jnp.int32, cos.shape, 1)
    first_half = (lane % QK_DIM) < (QK_DIM // 2)

    def rope(t, scale):
        outs = []
        for hh in range(N_HEADS):
            th = t[:, hh * HEAD_W:(hh + 1) * HEAD_W]
            up = pltpu.roll(th, HEAD_W - QK_DIM // 2, 1)
            dn = pltpu.roll(th, QK_DIM // 2, 1)
            partner = jnp.where(first_half, up, dn)
            outs.append(((th * cos + partner * sin) * scale).astype(BF16))
        return jnp.concatenate(outs, axis=1)

    dq_ref[...] = rope(seg(4), QK_DIM ** -0.5 * LOG2E)
    dk_ref[...] = rope(seg(5), 1.0)
    vt = seg(6).T.astype(BF16)
    for hh in range(N_HEADS):
        dvt_ref[hh, :HEAD_W, :] = vt[hh * HEAD_W:(hh + 1) * HEAD_W, :]
        dvt_ref[hh, HEAD_W:, :] = jnp.ones((VT_ROWS - HEAD_W, PROJ_TM), BF16)
    dgate_ref[...] = _silu(seg(7))


def _in_proj(x2, norm_w, w_in_bf, lb_logits, cos_t, sin_t):
    T, D = x2.shape
    n = T // PROJ_TM
    row = lambda i: (i, 0)
    fix = lambda i: (0, 0)
    f32_out = jax.ShapeDtypeStruct((T, GROUP_W), F32)
    bf_out = jax.ShapeDtypeStruct((T, GROUP_W), BF16)
    blk = pl.BlockSpec((PROJ_TM, GROUP_W), row)
    return pl.pallas_call(
        _in_proj_kernel,
        grid=(n,),
        in_specs=[
            pl.BlockSpec((PROJ_TM, D), row),
            pl.BlockSpec((1, D), fix),
            pl.BlockSpec(w_in_bf.shape, fix),
            pl.BlockSpec(lb_logits.shape, fix),
            pl.BlockSpec((QK_DIM // 2, PROJ_TM), lambda i: (0, i)),
            pl.BlockSpec((QK_DIM // 2, PROJ_TM), lambda i: (0, i)),
        ],
        out_specs=[blk, blk, blk, blk, blk, blk, blk,
                   pl.BlockSpec((N_HEADS, VT_ROWS, PROJ_TM), lambda i: (0, 0, i)), blk],
        out_shape=[f32_out, f32_out, f32_out, f32_out, f32_out, bf_out, bf_out,
                   jax.ShapeDtypeStruct((N_HEADS, VT_ROWS, T), BF16), f32_out],
        compiler_params=pltpu.CompilerParams(
            dimension_semantics=("arbitrary",), vmem_limit_bytes=VMEM_LIMIT),
        name="in_proj",
    )(x2, norm_w, w_in_bf, lb_logits, cos_t, sin_t)


def _hgrn_kernel(q_ref, k_ref, g_ref, v_ref, gate_ref, nw_ref, o_ref, st_ref, c_ref):
    C, SB = HGRN_CHUNK, HGRN_SUB

    @pl.when(pl.program_id(0) == 0)
    def _():
        st_ref[...] = jnp.zeros_like(st_ref)

    ti = lax.broadcasted_iota(jnp.int32, (C, C), 0)
    si = lax.broadcasted_iota(jnp.int32, (C, C), 1)
    tri = (si <= ti).astype(F32)
    a_row = lax.broadcasted_iota(jnp.int32, (SUBLANES, C), 0)
    a_lane = lax.broadcasted_iota(jnp.int32, (SUBLANES, C), 1)
    nw = nw_ref[...]

    def chunk(c, carry):
        r0 = pl.multiple_of(c * C, C)
        rows = pl.ds(r0, C)
        b_all = jnp.dot(tri, g_ref[rows, :], preferred_element_type=F32,
                        precision=lax.Precision.HIGHEST)
        c_ref[...] = b_all - jnp.log2(jnp.maximum(k_ref[rows, :], 0.0))
        for hh in range(N_HEADS):
            cols = slice(hh * HEAD_W, (hh + 1) * HEAD_W)
            b = b_all[:, cols]
            q = q_ref[rows, cols]
            k = k_ref[rows, cols]
            v = v_ref[rows, cols]
            v_bf = v.astype(BF16)
            st = st_ref[hh]
            inter = lax.dot_general((q * jnp.exp2(b)).astype(BF16), st.astype(BF16),
                                    NT, preferred_element_type=F32)
            b_last = b[C - 1:C, :]
            k_dec = (k * jnp.exp2(b_last - b)).astype(BF16)
            st_ref[hh] = st * jnp.exp2(b_last) + lax.dot_general(
                v_bf, k_dec, TN, preferred_element_type=F32)

            q_near, k_near = [], []
            for blk in range(C // SB):
                lo, mid = blk * SB, blk * SB + SUBLANES
                ref = b[mid - 1:mid, :]
                q_near.append(q[mid:mid + SUBLANES, :] * jnp.exp2(b[mid:mid + SUBLANES, :] - ref))
                k_near.append(k[lo:mid, :] * jnp.exp2(ref - b[lo:mid, :]))
                k_near.append(jnp.zeros((SUBLANES, HEAD_W), F32))
            a_near = lax.dot_general(jnp.concatenate(q_near, axis=0).astype(BF16),
                                     jnp.concatenate(k_near, axis=0).astype(BF16),
                                     NT, preferred_element_type=F32)
            pieces = []
            for blk in range(C // SB):
                lo = blk * SB
                near = a_near[blk * SUBLANES:(blk + 1) * SUBLANES, :]
                if blk > 0:
                    ref = b[lo - 1:lo, :]
                    q_t = (q[lo:lo + SB, :] * jnp.exp2(b[lo:lo + SB, :] - ref)).astype(BF16)
                    k_t = (k[:lo, :] * jnp.exp2(ref - b[:lo, :])).astype(BF16)
                    k_t = jnp.concatenate([k_t, jnp.zeros((C - lo, HEAD_W), BF16)], axis=0)
                    a_far = lax.dot_general(q_t, k_t, NT, preferred_element_type=F32)
                    halves = [a_far[:SUBLANES, :],
                              jnp.where(a_lane >= lo, near, a_far[SUBLANES:, :])]
                else:
                    halves = [jnp.zeros((SUBLANES, C), F32), near]
                for s in range(SB):
                    half = s // SUBLANES
                    r = lo + half * SUBLANES
                    c_s = c_ref[pl.ds(lo + s, 1), cols]
                    w = q[r:r + SUBLANES, :] * jnp.exp2(b[r:r + SUBLANES, :] - c_s)
                    col = jnp.sum(w, axis=-1, keepdims=True)
                    halves[half] = jnp.where(a_lane == lo + s, col, halves[half])
                for half in range(2):
                    keep = a_lane <= a_row + (lo + half * SUBLANES)
                    pieces.append(jnp.where(keep, halves[half], 0.0))
            a = jnp.concatenate(pieces, axis=0).astype(BF16)
            o = inter + jnp.dot(a, v_bf, preferred_element_type=F32)
            ms = jnp.mean(o * o, axis=-1, keepdims=True)
            o_ref[rows, cols] = (o * lax.rsqrt(ms + NORM_EPS) * nw
                                 * gate_ref[rows, cols]).astype(BF16)
        return carry

    lax.fori_loop(0, HGRN_TM // C, chunk, 0, unroll=HGRN_UNROLL)


def _hgrn(hq, hk, hgl, hi, hgate, norm_w):
    T = hq.shape[0]
    blk = pl.BlockSpec((HGRN_TM, GROUP_W), lambda i: (i, 0))
    return pl.pallas_call(
        _hgrn_kernel,
        grid=(T // HGRN_TM,),
        in_specs=[blk, blk, blk, blk, blk, pl.BlockSpec((1, HEAD_W), lambda i: (0, 0))],
        out_specs=blk,
        out_shape=jax.ShapeDtypeStruct((T, GROUP_W), BF16),
        scratch_shapes=[pltpu.VMEM((N_HEADS, HEAD_W, HEAD_W), F32),
                        pltpu.VMEM((HGRN_CHUNK, GROUP_W), F32)],
        compiler_params=pltpu.CompilerParams(
            dimension_semantics=("arbitrary",), vmem_limit_bytes=VMEM_LIMIT),
        name="hgrn2",
    )(hq, hk, hgl, hi, hgate, norm_w)


def _attn_kernel(lq1_ref, lk1_ref, lq2_ref, lk2_ref, q_ref, k_ref, vt_ref, nw_ref,
                 o_ref, s_ref, acc_ref, *, lambda_init):
    n_q = q_ref.shape[0] // ATT_TQ
    n_steps = n_q * (n_q + 1) // 2
    assert n_steps % ATT_STEPS == 0 and ATT_STEPS % 2 == 0 and ATT_TQ == ATT_TK
    n_t = ATT_TQ // ATT_TN
    lam = (jnp.exp(jnp.sum(lq1_ref[...] * lk1_ref[...]))
           - jnp.exp(jnp.sum(lq2_ref[...] * lk2_ref[...])) + lambda_init)
    lane = lax.broadcasted_iota(jnp.int32, (ATT_TQ, HEAD_W), 1)
    neg_inf = jnp.full((1, ATT_TN), -jnp.inf, F32)

    acc_ref[...] = jnp.zeros_like(acc_ref)

    def advance(pos):
        qi, t = pos
        row_end = t == qi
        qi_n = jnp.where(row_end, qi + 1, qi)
        t_n = jnp.where(row_end, 0, t + 1)
        done = qi_n == n_q
        return jnp.where(done, n_q - 1, qi_n), jnp.where(done, n_q - 1, t_n)

    def scores(pos, slot, masked):
        qi, t = pos
        q = q_ref[pl.ds(pl.multiple_of(qi * ATT_TQ, ATT_TQ), ATT_TQ), :]
        kb = k_ref[pl.ds(pl.multiple_of(t * ATT_TK, ATT_TK), ATT_TK), :]
        zero = jnp.zeros_like(q)
        q_maps = (jnp.where(lane < QK_DIM, q, zero), jnp.where(lane >= QK_DIM, q, zero))
        mx = []
        for c in range(2):
            for n in range(n_t):
                cols = slice(n * ATT_TN, (n + 1) * ATT_TN)
                s = lax.dot_general(kb, q_maps[c][cols, :], NT,
                                    preferred_element_type=F32)
                if masked:
                    kpos = lax.broadcasted_iota(jnp.int32, s.shape, 0) + t * ATT_TK
                    qpos = lax.broadcasted_iota(jnp.int32, s.shape, 1) + (qi * ATT_TQ + n * ATT_TN)
                    s = jnp.where(kpos <= qpos, s, -jnp.inf)
                s_ref[slot, c, :, cols] = s
                mx.append(jnp.max(s, axis=0, keepdims=True))
        return tuple(mx)

    def softmax_pv(pos, slot, mx, m_run):
        qi, t = pos
        first = t == 0
        a = qi % ATT_STEPS
        vt = vt_ref[:, pl.ds(pl.multiple_of(t * ATT_TK, ATT_TK), ATT_TK)]
        m_out = []
        for c in range(2):
            for n in range(n_t):
                i = c * n_t + n
                cols = slice(n * ATT_TN, (n + 1) * ATT_TN)
                m_old = jnp.where(first, neg_inf, m_run[i])
                m_new = jnp.maximum(m_old, mx[i])
                alpha = jnp.exp2(m_old - m_new)
                p = jnp.exp2(s_ref[slot, c, :, cols] - m_new).astype(BF16)
                upd = jnp.dot(vt, p, preferred_element_type=F32)
                acc_ref[a, c, :, cols] = alpha * acc_ref[a, c, :, cols] + upd
                m_out.append(m_new)
        return tuple(m_out)

    def finalize(qi):
        a1 = acc_ref[qi % ATT_STEPS, 0]
        a2 = acc_ref[qi % ATT_STEPS, 1]
        o_t = (a1[:HEAD_W] / a1[HEAD_W:HEAD_W + 1]
               - lam * (a2[:HEAD_W] / a2[HEAD_W:HEAD_W + 1]))
        o = o_t.T
        ms = jnp.mean(o * o, axis=-1, keepdims=True)
        rows = pl.ds(pl.multiple_of(qi * ATT_TQ, ATT_TQ), ATT_TQ)
        o_ref[rows, :] = o * lax.rsqrt(ms + SUBLN_EPS) * nw_ref[...] * (1.0 - lambda_init)

    def on_diag(pos):
        return pos[0] == pos[1]

    def body_block(pos_a, pos_b, mx, m_run, masked_step):
        for j in range(ATT_STEPS):
            mx_next = scores(pos_a[j], (j + 1) % 2, j == masked_step)
            m_run = softmax_pv(pos_b[j], j % 2, mx, m_run)
            mx = mx_next
        return mx, m_run

    def positions(pos):
        out = [pos]
        for _ in range(ATT_STEPS - 1):
            out.append(advance(out[-1]))
        return out

    def body(it, carry):
        pos_a0, pos_b0, mx, m_run = carry
        pos_a, pos_b = positions(pos_a0), positions(pos_b0)
        idx = jnp.int32(0)
        for j in range(ATT_STEPS):
            real = it * ATT_STEPS + 1 + j < n_steps
            idx = idx + (j + 1) * (on_diag(pos_a[j]) & real).astype(jnp.int32)
        branches = [functools.partial(lambda ms, _: body_block(pos_a, pos_b, mx, m_run, ms), ms)
                    for ms in [None] + list(range(ATT_STEPS))]
        mx, m_run = lax.switch(idx, branches, 0)

        for j in range(ATT_STEPS):
            @pl.when(on_diag(pos_b[j]))
            def _():
                finalize(pos_b[j][0])

        return advance(pos_a[-1]), advance(pos_b[-1]), mx, m_run

    pos0 = (jnp.int32(0), jnp.int32(0))
    mx0 = scores(pos0, 0, True)
    m0 = (neg_inf,) * (2 * n_t)
    lax.fori_loop(0, n_steps // ATT_STEPS, body, (advance(pos0), pos0, mx0, m0))


def _diff_attn(lq1, lk1, lq2, lk2, dq, dk, dvt, norm_w, lambda_init):
    T = dq.shape[0]
    lam_spec = pl.BlockSpec((1, QK_DIM), lambda h: (0, 0))
    return pl.pallas_call(
        functools.partial(_attn_kernel, lambda_init=lambda_init),
        grid=(N_HEADS,),
        in_specs=[
            lam_spec, lam_spec, lam_spec, lam_spec,
            pl.BlockSpec((T, HEAD_W), lambda h: (0, h)),
            pl.BlockSpec((T, HEAD_W), lambda h: (0, h)),
            pl.BlockSpec((None, VT_ROWS, T), lambda h: (h, 0, 0)),
            pl.BlockSpec((1, HEAD_W), lambda h: (0, 0)),
        ],
        out_specs=pl.BlockSpec((T, HEAD_W), lambda h: (0, h)),
        out_shape=jax.ShapeDtypeStruct((T, GROUP_W), F32),
        scratch_shapes=[
            pltpu.VMEM((2, 2, ATT_TK, ATT_TQ), F32),
            pltpu.VMEM((ATT_STEPS, 2, VT_ROWS, ATT_TQ), F32),
        ],
        compiler_params=pltpu.CompilerParams(
            dimension_semantics=("arbitrary",), vmem_limit_bytes=VMEM_LIMIT),
        name="diff_attn",
    )(lq1, lk1, lq2, lk2, dq, dk, dvt, norm_w)


def _out_proj_kernel(ho_ref, do_ref, dgate_ref, x_ref, w_ref, nw_ref, o_ref):
    y = x_ref[...]
    y = y + jnp.dot(ho_ref[...], w_ref[:GROUP_W, :], preferred_element_type=F32)
    mix_d = (do_ref[...] * dgate_ref[...]).astype(BF16)
    y = y + jnp.dot(mix_d, w_ref[GROUP_W:, :], preferred_element_type=F32)
    ms = jnp.mean(y * y, axis=-1, keepdims=True)
    o_ref[...] = y * lax.rsqrt(ms + NORM_EPS) * nw_ref[...]


def _out_proj(ho, do, dgate, x2, w_out_bf, final_norm_w):
    T, D = x2.shape
    row = lambda i: (i, 0)
    fix = lambda i: (0, 0)
    return pl.pallas_call(
        _out_proj_kernel,
        grid=(T // PROJ_TM,),
        in_specs=[
            pl.BlockSpec((PROJ_TM, GROUP_W), row),
            pl.BlockSpec((PROJ_TM, GROUP_W), row),
            pl.BlockSpec((PROJ_TM, GROUP_W), row),
            pl.BlockSpec((PROJ_TM, D), row),
            pl.BlockSpec(w_out_bf.shape, fix),
            pl.BlockSpec((1, D), fix),
        ],
        out_specs=pl.BlockSpec((PROJ_TM, D), row),
        out_shape=jax.ShapeDtypeStruct((T, D), F32),
        compiler_params=pltpu.CompilerParams(
            dimension_semantics=("arbitrary",), vmem_limit_bytes=VMEM_LIMIT),
        name="out_proj",
    )(ho, do, dgate, x2, w_out_bf, final_norm_w)


def _rope_tables(T):
    half = QK_DIM // 2
    inv_freq = 1.0 / (ROPE_THETA ** (jnp.arange(half, dtype=F32) / half))
    ang = inv_freq[:, None] * jnp.arange(T, dtype=F32)[None, :]
    return jnp.cos(ang), jnp.sin(ang)


def kernel(x, norm_w, w_in, hgrn_lb_logits, hgrn_norm_w, diff_lambda_q1, diff_lambda_k1,
           diff_lambda_q2, diff_lambda_k2, diff_norm_w, w_out, final_norm_w):
    B, T, D = x.shape
    depth = norm_w.shape[0]
    assert B == 1 and depth == 1 and D == 2 * GROUP_W
    assert T % ATT_TQ == 0 and T % HGRN_TM == 0 and T % PROJ_TM == 0
    x2 = x.reshape(T, D)
    cos_t, sin_t = _rope_tables(T)
    l = 0
    lambda_init = 0.8 - 0.6 * math.exp(-0.3 * l)
    hq, hk, hgl, hi, hgate, dq, dk, dvt, dgate = _in_proj(
        x2, norm_w[l:l + 1], w_in[l].astype(BF16), hgrn_lb_logits, cos_t, sin_t)
    ho = _hgrn(hq, hk, hgl, hi, hgate, hgrn_norm_w[l:l + 1])
    do = _diff_attn(diff_lambda_q1[l:l + 1], diff_lambda_k1[l:l + 1],
                    diff_lambda_q2[l:l + 1], diff_lambda_k2[l:l + 1],
                    dq, dk, dvt, diff_norm_w[l:l + 1], lambda_init)
    out = _out_proj(ho, do, dgate, x2, w_out[l].astype(BF16), final_norm_w.reshape(1, D))
    return out.reshape(B, T, D)
```

```python
import functools
import math

import jax
import jax.numpy as jnp
from jax import lax
from jax.experimental import pallas as pl
from jax.experimental.pallas import tpu as pltpu

F32 = jnp.float32
BF16 = jnp.bfloat16

SUBLANES = 8
VMEM_LIMIT = 56 * 1024 * 1024

N_HEADS = 4
HEAD_W = 128
GROUP_W = N_HEADS * HEAD_W
QK_DIM = 64
ROPE_THETA = 10000.0
NORM_EPS = 1e-6
SUBLN_EPS = 1e-5

IN_TM = 1024
PROJ_TM = 512
HGRN_CHUNK = 64
HGRN_SUB = 16
HGRN_TM = 512
HGRN_UNROLL = 8
ATT_TQ = 512
ATT_TK = 512
ATT_TN = 256
ATT_STEPS = 4
VT_ROWS = HEAD_W + 16
LOG2E = math.log2(math.e)

NT = (((1,), (1,)), ((), ()))
TN = (((0,), (0,)), ((), ()))


def _run_interleaved(*gens):
    results = [None] * len(gens)
    live = list(range(len(gens)))
    while live:
        for i in list(live):
            try:
                next(gens[i])
            except StopIteration as stop:
                results[i] = stop.value
                live.remove(i)
    return results


def _silu(x):
    return x * (1.0 / (1.0 + jnp.exp(-x)))


def _in_proj_kernel(x_ref, nw_ref, w_ref, lbl_ref, cos_ref, sin_ref,
                    hq_ref, hk_ref, hgl_ref, hi_ref, hgate_ref,
                    dq_ref, dk_ref, dvt_ref, dgate_ref):
    x = x_ref[...]
    ms = jnp.mean(x * x, axis=-1, keepdims=True)
    h = (x * lax.rsqrt(ms + NORM_EPS) * nw_ref[...]).astype(BF16)

    def seg(i):
        return jnp.dot(h, w_ref[:, i * GROUP_W:(i + 1) * GROUP_W],
                       preferred_element_type=F32)

    lg = lbl_ref[...]
    e = jnp.exp(lg - jnp.max(lg, axis=0, keepdims=True))
    lb = e[0:1, :] / jnp.sum(e, axis=0, keepdims=True)

    hq_ref[...] = seg(0)
    f = lb + (1.0 - lb) * (1.0 / (1.0 + jnp.exp(-seg(1))))
    hk_ref[...] = 1.0 - f
    hgl_ref[...] = jnp.log2(f)
    hi_ref[...] = seg(2)
    hgate_ref[...] = _silu(seg(3))

    reps = HEAD_W // (QK_DIM // 2)
    cos = jnp.concatenate([cos_ref[...]] * reps, axis=0).T
    sin_half = sin_ref[...]
    sin = jnp.concatenate([-sin_half, sin_half] * (reps // 2), axis=0).T
    lane = lax.broadcasted_iota(jnp.int32, cos.shape, 1)
    first_half = (lane % QK_DIM) < (QK_DIM // 2)

    def rope(t, scale):
        outs = []
        for hh in range(N_HEADS):
            th = t[:, hh * HEAD_W:(hh + 1) * HEAD_W]
            up = pltpu.roll(th, HEAD_W - QK_DIM // 2, 1)
            dn = pltpu.roll(th, QK_DIM // 2, 1)
            partner = jnp.where(first_half, up, dn)
            outs.append(((th * cos + partner * sin) * scale).astype(BF16))
        return jnp.concatenate(outs, axis=1)

    dq_ref[...] = rope(seg(4), QK_DIM ** -0.5 * LOG2E)
    dk_ref[...] = rope(seg(5), 1.0)
    vt = seg(6).T.astype(BF16)
    for hh in range(N_HEADS):
        dvt_ref[hh, :HEAD_W, :] = vt[hh * HEAD_W:(hh + 1) * HEAD_W, :]
        dvt_ref[hh, HEAD_W:, :] = jnp.ones((VT_ROWS - HEAD_W, IN_TM), BF16)
    dgate_ref[...] = _silu(seg(7))


def _in_proj(x2, norm_w, w_in_bf, lb_logits, cos_t, sin_t):
    T, D = x2.shape
    n = T // IN_TM
    row = lambda i: (i, 0)
    fix = lambda i: (0, 0)
    f32_out = jax.ShapeDtypeStruct((T, GROUP_W), F32)
    bf_out = jax.ShapeDtypeStruct((T, GROUP_W), BF16)
    blk = pl.BlockSpec((IN_TM, GROUP_W), row)
    return pl.pallas_call(
        _in_proj_kernel,
        grid=(n,),
        in_specs=[
            pl.BlockSpec((IN_TM, D), row),
            pl.BlockSpec((1, D), fix),
            pl.BlockSpec(w_in_bf.shape, fix, pipeline_mode=pl.Buffered(1)),
            pl.BlockSpec(lb_logits.shape, fix),
            pl.BlockSpec((QK_DIM // 2, IN_TM), lambda i: (0, i)),
            pl.BlockSpec((QK_DIM // 2, IN_TM), lambda i: (0, i)),
        ],
        out_specs=[blk, blk, blk, blk, blk, blk, blk,
                   pl.BlockSpec((N_HEADS, VT_ROWS, IN_TM), lambda i: (0, 0, i)), blk],
        out_shape=[f32_out, f32_out, f32_out, f32_out, f32_out, bf_out, bf_out,
                   jax.ShapeDtypeStruct((N_HEADS, VT_ROWS, T), BF16), f32_out],
        compiler_params=pltpu.CompilerParams(
            dimension_semantics=("arbitrary",), vmem_limit_bytes=VMEM_LIMIT),
        name="in_proj",
    )(x2, norm_w, w_in_bf, lb_logits, cos_t, sin_t)


def _hgrn_kernel(q_ref, k_ref, g_ref, v_ref, gate_ref, nw_ref, o_ref, st_ref, c_ref):
    C, SB = HGRN_CHUNK, HGRN_SUB

    @pl.when(pl.program_id(0) == 0)
    def _():
        st_ref[...] = jnp.zeros_like(st_ref)

    ti = lax.broadcasted_iota(jnp.int32, (C, C), 0)
    si = lax.broadcasted_iota(jnp.int32, (C, C), 1)
    tri = (si <= ti).astype(F32)
    a_row = lax.broadcasted_iota(jnp.int32, (SUBLANES, C), 0)
    a_lane = lax.broadcasted_iota(jnp.int32, (SUBLANES, C), 1)
    nw = nw_ref[...]

    def chunk(c, carry):
        r0 = pl.multiple_of(c * C, C)
        rows = pl.ds(r0, C)
        b_all = jnp.dot(tri, g_ref[rows, :], preferred_element_type=F32,
                        precision=lax.Precision.HIGHEST)
        c_ref[...] = b_all - jnp.log2(jnp.maximum(k_ref[rows, :], 0.0))
        for hh in range(N_HEADS):
            cols = slice(hh * HEAD_W, (hh + 1) * HEAD_W)
            b = b_all[:, cols]
            q = q_ref[rows, cols]
            k = k_ref[rows, cols]
            v = v_ref[rows, cols]
            v_bf = v.astype(BF16)
            st = st_ref[hh]
            inter = lax.dot_general((q * jnp.exp2(b)).astype(BF16), st.astype(BF16),
                                    NT, preferred_element_type=F32)
            b_last = b[C - 1:C, :]
            k_dec = (k * jnp.exp2(b_last - b)).astype(BF16)
            st_ref[hh] = st * jnp.exp2(b_last) + lax.dot_general(
                v_bf, k_dec, TN, preferred_element_type=F32)

            q_near, k_near = [], []
            for blk in range(C // SB):
                lo, mid = blk * SB, blk * SB + SUBLANES
                ref = b[mid - 1:mid, :]
                q_near.append(q[mid:mid + SUBLANES, :] * jnp.exp2(b[mid:mid + SUBLANES, :] - ref))
                k_near.append(k[lo:mid, :] * jnp.exp2(ref - b[lo:mid, :]))
                k_near.append(jnp.zeros((SUBLANES, HEAD_W), F32))
            a_near = lax.dot_general(jnp.concatenate(q_near, axis=0).astype(BF16),
                                     jnp.concatenate(k_near, axis=0).astype(BF16),
                                     NT, preferred_element_type=F32)
            pieces = []
            for blk in range(C // SB):
                lo = blk * SB
                near = a_near[blk * SUBLANES:(blk + 1) * SUBLANES, :]
                if blk > 0:
                    ref = b[lo - 1:lo, :]
                    q_t = (q[lo:lo + SB, :] * jnp.exp2(b[lo:lo + SB, :] - ref)).astype(BF16)
                    k_t = (k[:lo, :] * jnp.exp2(ref - b[:lo, :])).astype(BF16)
                    k_t = jnp.concatenate([k_t, jnp.zeros((C - lo, HEAD_W), BF16)], axis=0)
                    a_far = lax.dot_general(q_t, k_t, NT, preferred_element_type=F32)
                    halves = [a_far[:SUBLANES, :],
                              jnp.where(a_lane >= lo, near, a_far[SUBLANES:, :])]
                else:
                    halves = [jnp.zeros((SUBLANES, C), F32), near]
                for s in range(SB):
                    half = s // SUBLANES
                    r = lo + half * SUBLANES
                    c_s = c_ref[pl.ds(lo + s, 1), cols]
                    w = q[r:r + SUBLANES, :] * jnp.exp2(b[r:r + SUBLANES, :] - c_s)
                    col = jnp.sum(w, axis=-1, keepdims=True)
                    halves[half] = jnp.where(a_lane == lo + s, col, halves[half])
                for half in range(2):
                    keep = a_lane <= a_row + (lo + half * SUBLANES)
                    pieces.append(jnp.where(keep, halves[half], 0.0))
            a = jnp.concatenate(pieces, axis=0).astype(BF16)
            o = inter + jnp.dot(a, v_bf, preferred_element_type=F32)
            ms = jnp.mean(o * o, axis=-1, keepdims=True)
            o_ref[rows, cols] = (o * lax.rsqrt(ms + NORM_EPS) * nw
                                 * gate_ref[rows, cols]).astype(BF16)
        return carry

    lax.fori_loop(0, HGRN_TM // C, chunk, 0, unroll=HGRN_UNROLL)


def _hgrn(hq, hk, hgl, hi, hgate, norm_w):
    T = hq.shape[0]
    blk = pl.BlockSpec((HGRN_TM, GROUP_W), lambda i: (i, 0))
    return pl.pallas_call(
        _hgrn_kernel,
        grid=(T // HGRN_TM,),
        in_specs=[blk, blk, blk, blk, blk, pl.BlockSpec((1, HEAD_W), lambda i: (0, 0))],
        out_specs=blk,
        out_shape=jax.ShapeDtypeStruct((T, GROUP_W), BF16),
        scratch_shapes=[pltpu.VMEM((N_HEADS, HEAD_W, HEAD_W), F32),
                        pltpu.VMEM((HGRN_CHUNK, GROUP_W), F32)],
        compiler_params=pltpu.CompilerParams(
            dimension_semantics=("arbitrary",), vmem_limit_bytes=VMEM_LIMIT),
        name="hgrn2",
    )(hq, hk, hgl, hi, hgate, norm_w)


def _attn_kernel(lq1_ref, lk1_ref, lq2_ref, lk2_ref, q_ref, k_ref, vt_ref, nw_ref,
                 o_ref, s_ref, acc_ref, *, lambda_init):
    n_q = q_ref.shape[0] // ATT_TQ
    n_steps = n_q * (n_q + 1) // 2
    assert n_steps % ATT_STEPS == 0 and ATT_STEPS % 2 == 0 and ATT_TQ == ATT_TK
    n_t = ATT_TQ // ATT_TN
    lam = (jnp.exp(jnp.sum(lq1_ref[...] * lk1_ref[...]))
           - jnp.exp(jnp.sum(lq2_ref[...] * lk2_ref[...])) + lambda_init)
    lane = lax.broadcasted_iota(jnp.int32, (ATT_TQ, HEAD_W), 1)
    neg_inf = jnp.full((1, ATT_TN), -jnp.inf, F32)

    acc_ref[...] = jnp.zeros_like(acc_ref)

    def advance(pos):
        qi, t = pos
        row_end = t == qi
        qi_n = jnp.where(row_end, qi + 1, qi)
        t_n = jnp.where(row_end, 0, t + 1)
        done = qi_n == n_q
        return jnp.where(done, n_q - 1, qi_n), jnp.where(done, n_q - 1, t_n)

    def scores(pos, slot, masked):
        qi, t = pos
        q = q_ref[pl.ds(pl.multiple_of(qi * ATT_TQ, ATT_TQ), ATT_TQ), :]
        kb = k_ref[pl.ds(pl.multiple_of(t * ATT_TK, ATT_TK), ATT_TK), :]
        zero = jnp.zeros_like(q)
        q_maps = (jnp.where(lane < QK_DIM, q, zero), jnp.where(lane >= QK_DIM, q, zero))
        mx = []
        for c in range(2):
            for n in range(n_t):
                cols = slice(n * ATT_TN, (n + 1) * ATT_TN)
                s = lax.dot_general(kb, q_maps[c][cols, :], NT,
                                    preferred_element_type=F32)
                if masked:
                    kpos = lax.broadcasted_iota(jnp.int32, s.shape, 0) + t * ATT_TK
                    qpos = lax.broadcasted_iota(jnp.int32, s.shape, 1) + (qi * ATT_TQ + n * ATT_TN)
                    s = jnp.where(kpos <= qpos, s, -jnp.inf)
                s_ref[slot, c, :, cols] = s
                mx.append(jnp.max(s, axis=0, keepdims=True))
                yield
        return tuple(mx)

    def softmax_pv(pos, slot, mx, m_run):
        qi, t = pos
        first = t == 0
        a = qi % ATT_STEPS
        vt = vt_ref[:, pl.ds(pl.multiple_of(t * ATT_TK, ATT_TK), ATT_TK)]
        m_out = []
        for c in range(2):
            for n in range(n_t):
                i = c * n_t + n
                cols = slice(n * ATT_TN, (n + 1) * ATT_TN)
                m_old = jnp.where(first, neg_inf, m_run[i])
                m_new = jnp.maximum(m_old, mx[i])
                alpha = jnp.exp2(m_old - m_new)
                p = jnp.exp2(s_ref[slot, c, :, cols] - m_new).astype(BF16)
                upd = jnp.dot(vt, p, preferred_element_type=F32)
                acc_ref[a, c, :, cols] = alpha * acc_ref[a, c, :, cols] + upd
                m_out.append(m_new)
                yield
        return tuple(m_out)

    def finalize(qi):
        a1 = acc_ref[qi % ATT_STEPS, 0]
        a2 = acc_ref[qi % ATT_STEPS, 1]
        o_t = (a1[:HEAD_W] / a1[HEAD_W:HEAD_W + 1]
               - lam * (a2[:HEAD_W] / a2[HEAD_W:HEAD_W + 1]))
        o = o_t.T
        ms = jnp.mean(o * o, axis=-1, keepdims=True)
        rows = pl.ds(pl.multiple_of(qi * ATT_TQ, ATT_TQ), ATT_TQ)
        o_ref[rows, :] = o * lax.rsqrt(ms + SUBLN_EPS) * nw_ref[...] * (1.0 - lambda_init)

    def on_diag(pos):
        return pos[0] == pos[1]

    def body_block(pos_a, pos_b, mx, m_run, masked_step):
        for j in range(ATT_STEPS):
            mx, m_run = _run_interleaved(scores(pos_a[j], (j + 1) % 2, j == masked_step),
                                         softmax_pv(pos_b[j], j % 2, mx, m_run))
        return mx, m_run

    def positions(pos):
        out = [pos]
        for _ in range(ATT_STEPS - 1):
            out.append(advance(out[-1]))
        return out

    def body(it, carry):
        pos_a0, pos_b0, mx, m_run = carry
        pos_a, pos_b = positions(pos_a0), positions(pos_b0)
        idx = jnp.int32(0)
        for j in range(ATT_STEPS):
            real = it * ATT_STEPS + 1 + j < n_steps
            idx = idx + (j + 1) * (on_diag(pos_a[j]) & real).astype(jnp.int32)
        branches = [functools.partial(lambda ms, _: body_block(pos_a, pos_b, mx, m_run, ms), ms)
                    for ms in [None] + list(range(ATT_STEPS))]
        mx, m_run = lax.switch(idx, branches, 0)

        for j in range(ATT_STEPS):
            @pl.when(on_diag(pos_b[j]))
            def _():
                finalize(pos_b[j][0])

        return advance(pos_a[-1]), advance(pos_b[-1]), mx, m_run

    pos0 = (jnp.int32(0), jnp.int32(0))
    mx0 = _run_interleaved(scores(pos0, 0, True))[0]
    m0 = (neg_inf,) * (2 * n_t)
    lax.fori_loop(0, n_steps // ATT_STEPS, body, (advance(pos0), pos0, mx0, m0))


def _diff_attn(lq1, lk1, lq2, lk2, dq, dk, dvt, norm_w, lambda_init):
    T = dq.shape[0]
    lam_spec = pl.BlockSpec((1, QK_DIM), lambda h: (0, 0))
    return pl.pallas_call(
        functools.partial(_attn_kernel, lambda_init=lambda_init),
        grid=(N_HEADS,),
        in_specs=[
            lam_spec, lam_spec, lam_spec, lam_spec,
            pl.BlockSpec((T, HEAD_W), lambda h: (0, h)),
            pl.BlockSpec((T, HEAD_W), lambda h: (0, h)),
            pl.BlockSpec((None, VT_ROWS, T), lambda h: (h, 0, 0)),
            pl.BlockSpec((1, HEAD_W), lambda h: (0, 0)),
        ],
        out_specs=pl.BlockSpec((T, HEAD_W), lambda h: (0, h)),
        out_shape=jax.ShapeDtypeStruct((T, GROUP_W), F32),
        scratch_shapes=[
            pltpu.VMEM((2, 2, ATT_TK, ATT_TQ), F32),
            pltpu.VMEM((ATT_STEPS, 2, VT_ROWS, ATT_TQ), F32),
        ],
        compiler_params=pltpu.CompilerParams(
            dimension_semantics=("arbitrary",), vmem_limit_bytes=VMEM_LIMIT),
        name="diff_attn",
    )(lq1, lk1, lq2, lk2, dq, dk, dvt, norm_w)


def _out_proj_kernel(ho_ref, do_ref, dgate_ref, x_ref, w_ref, nw_ref, o_ref):
    y = x_ref[...]
    y = y + jnp.dot(ho_ref[...], w_ref[:GROUP_W, :], preferred_element_type=F32)
    mix_d = (do_ref[...] * dgate_ref[...]).astype(BF16)
    y = y + jnp.dot(mix_d, w_ref[GROUP_W:, :], preferred_element_type=F32)
    ms = jnp.mean(y * y, axis=-1, keepdims=True)
    o_ref[...] = y * lax.rsqrt(ms + NORM_EPS) * nw_ref[...]


def _out_proj(ho, do, dgate, x2, w_out_bf, final_norm_w):
    T, D = x2.shape
    row = lambda i: (i, 0)
    fix = lambda i: (0, 0)
    return pl.pallas_call(
        _out_proj_kernel,
        grid=(T // PROJ_TM,),
        in_specs=[
            pl.BlockSpec((PROJ_TM, GROUP_W), row),
            pl.BlockSpec((PROJ_TM, GROUP_W), row),
            pl.BlockSpec((PROJ_TM, GROUP_W), row),
            pl.BlockSpec((PROJ_TM, D), row),
            pl.BlockSpec(w_out_bf.shape, fix),
            pl.BlockSpec((1, D), fix),
        ],
        out_specs=pl.BlockSpec((PROJ_TM, D), row),
        out_shape=jax.ShapeDtypeStruct((T, D), F32),
        compiler_params=pltpu.CompilerParams(
            dimension_semantics=("arbitrary",), vmem_limit_bytes=VMEM_LIMIT),
        name="out_proj",
    )(ho, do, dgate, x2, w_out_bf, final_norm_w)


def _rope_tables(T):
    half = QK_DIM // 2
    inv_freq = 1.0 / (ROPE_THETA ** (jnp.arange(half, dtype=F32) / half))
    ang = inv_freq[:, None] * jnp.arange(T, dtype=F32)[None, :]
    return jnp.cos(ang), jnp.sin(ang)


def kernel(x, norm_w, w_in, hgrn_lb_logits, hgrn_norm_w, diff_lambda_q1, diff_lambda_k1,
           diff_lambda_q2, diff_lambda_k2, diff_norm_w, w_out, final_norm_w):
    B, T, D = x.shape
    depth = norm_w.shape[0]
    assert B == 1 and depth == 1 and D == 2 * GROUP_W
    assert T % ATT_TQ == 0 and T % HGRN_TM == 0 and T % PROJ_TM == 0 and T % IN_TM == 0
    x2 = x.reshape(T, D)
    cos_t, sin_t = _rope_tables(T)
    l = 0
    lambda_init = 0.8 - 0.6 * math.exp(-0.3 * l)
    hq, hk, hgl, hi, hgate, dq, dk, dvt, dgate = _in_proj(
        x2, norm_w[l:l + 1], w_in[l].astype(BF16), hgrn_lb_logits, cos_t, sin_t)
    ho = _hgrn(hq, hk, hgl, hi, hgate, hgrn_norm_w[l:l + 1])
    do = _diff_attn(diff_lambda_q1[l:l + 1], diff_lambda_k1[l:l + 1],
                    diff_lambda_q2[l:l + 1], diff_lambda_k2[l:l + 1],
                    dq, dk, dvt, diff_norm_w[l:l + 1], lambda_init)
    out = _out_proj(ho, do, dgate, x2, w_out[l].astype(BF16), final_norm_w.reshape(1, D))
    return out.reshape(B, T, D)
```

```python
import functools
import math

import jax
import jax.numpy as jnp
from jax import lax
from jax.experimental import pallas as pl
from jax.experimental.pallas import tpu as pltpu

F32 = jnp.float32
BF16 = jnp.bfloat16

SUBLANES = 8
VMEM_LIMIT = 56 * 1024 * 1024

N_HEADS = 4
HEAD_W = 128
GROUP_W = N_HEADS * HEAD_W
QK_DIM = 64
ROPE_THETA = 10000.0
NORM_EPS = 1e-6
SUBLN_EPS = 1e-5

IN_TM = 1024
PROJ_TM = 512
HGRN_CHUNK = 64
HGRN_SUB = 16
HGRN_TM = 512
HGRN_GROUP = 8
ATT_TQ = 512
ATT_TK = 512
ATT_TN = 256
ATT_STEPS = 4
VT_ROWS = HEAD_W + 16
LOG2E = math.log2(math.e)

NT = (((1,), (1,)), ((), ()))
TN = (((0,), (0,)), ((), ()))


def _run_interleaved(*gens):
    results = [None] * len(gens)
    live = list(range(len(gens)))
    while live:
        for i in list(live):
            try:
                next(gens[i])
            except StopIteration as stop:
                results[i] = stop.value
                live.remove(i)
    return results


def _silu(x):
    return x * (1.0 / (1.0 + jnp.exp(-x)))


def _in_proj_kernel(x_ref, nw_ref, w_ref, lbl_ref, cos_ref, sin_ref,
                    hq_ref, hk_ref, hgl_ref, hi_ref, hgate_ref,
                    dq_ref, dk_ref, dvt_ref, dgate_ref):
    x = x_ref[...]
    ms = jnp.mean(x * x, axis=-1, keepdims=True)
    h = (x * lax.rsqrt(ms + NORM_EPS) * nw_ref[...]).astype(BF16)

    def seg(i):
        return jnp.dot(h, w_ref[:, i * GROUP_W:(i + 1) * GROUP_W],
                       preferred_element_type=F32)

    lg = lbl_ref[...]
    e = jnp.exp(lg - jnp.max(lg, axis=0, keepdims=True))
    lb = e[0:1, :] / jnp.sum(e, axis=0, keepdims=True)

    hq_ref[...] = seg(0)
    f = lb + (1.0 - lb) * (1.0 / (1.0 + jnp.exp(-seg(1))))
    hk_ref[...] = 1.0 - f
    hgl_ref[...] = jnp.log2(f)
    hi_ref[...] = seg(2)
    hgate_ref[...] = _silu(seg(3))

    reps = HEAD_W // (QK_DIM // 2)
    cos = jnp.concatenate([cos_ref[...]] * reps, axis=0).T
    sin_half = sin_ref[...]
    sin = jnp.concatenate([-sin_half, sin_half] * (reps // 2), axis=0).T
    lane = lax.broadcasted_iota(jnp.int32, cos.shape, 1)
    first_half = (lane % QK_DIM) < (QK_DIM // 2)

    def rope(t, scale):
        outs = []
        for hh in range(N_HEADS):
            th = t[:, hh * HEAD_W:(hh + 1) * HEAD_W]
            up = pltpu.roll(th, HEAD_W - QK_DIM // 2, 1)
            dn = pltpu.roll(th, QK_DIM // 2, 1)
            partner = jnp.where(first_half, up, dn)
            outs.append(((th * cos + partner * sin) * scale).astype(BF16))
        return jnp.concatenate(outs, axis=1)

    dq_ref[...] = rope(seg(4), QK_DIM ** -0.5 * LOG2E)
    dk_ref[...] = rope(seg(5), 1.0)
    vt = seg(6).T.astype(BF16)
    for hh in range(N_HEADS):
        dvt_ref[hh, :HEAD_W, :] = vt[hh * HEAD_W:(hh + 1) * HEAD_W, :]
        dvt_ref[hh, HEAD_W:, :] = jnp.ones((VT_ROWS - HEAD_W, IN_TM), BF16)
    dgate_ref[...] = _silu(seg(7))


def _in_proj(x2, norm_w, w_in_bf, lb_logits, cos_t, sin_t):
    T, D = x2.shape
    n = T // IN_TM
    row = lambda i: (i, 0)
    fix = lambda i: (0, 0)
    f32_out = jax.ShapeDtypeStruct((T, GROUP_W), F32)
    bf_out = jax.ShapeDtypeStruct((T, GROUP_W), BF16)
    blk = pl.BlockSpec((IN_TM, GROUP_W), row)
    return pl.pallas_call(
        _in_proj_kernel,
        grid=(n,),
        in_specs=[
            pl.BlockSpec((IN_TM, D), row),
            pl.BlockSpec((1, D), fix),
            pl.BlockSpec(w_in_bf.shape, fix, pipeline_mode=pl.Buffered(1)),
            pl.BlockSpec(lb_logits.shape, fix),
            pl.BlockSpec((QK_DIM // 2, IN_TM), lambda i: (0, i)),
            pl.BlockSpec((QK_DIM // 2, IN_TM), lambda i: (0, i)),
        ],
        out_specs=[blk, blk, blk, blk, blk, blk, blk,
                   pl.BlockSpec((N_HEADS, VT_ROWS, IN_TM), lambda i: (0, 0, i)), blk],
        out_shape=[f32_out, f32_out, f32_out, f32_out, f32_out, bf_out, bf_out,
                   jax.ShapeDtypeStruct((N_HEADS, VT_ROWS, T), BF16), f32_out],
        compiler_params=pltpu.CompilerParams(
            dimension_semantics=("arbitrary",), vmem_limit_bytes=VMEM_LIMIT),
        name="in_proj",
    )(x2, norm_w, w_in_bf, lb_logits, cos_t, sin_t)


def _hgrn_kernel(q_ref, k_ref, g_ref, v_ref, gate_ref, nw_ref, o_ref, st_ref, c_ref):
    C, SB = HGRN_CHUNK, HGRN_SUB

    @pl.when(pl.program_id(0) == 0)
    def _():
        st_ref[...] = jnp.zeros_like(st_ref)

    ti = lax.broadcasted_iota(jnp.int32, (C, C), 0)
    si = lax.broadcasted_iota(jnp.int32, (C, C), 1)
    tri = (si <= ti).astype(F32)
    a_row = lax.broadcasted_iota(jnp.int32, (SUBLANES, C), 0)
    a_lane = lax.broadcasted_iota(jnp.int32, (SUBLANES, C), 1)
    nw = nw_ref[...]

    def chunk_heads(c, slot):
        r0 = pl.multiple_of(c * C, C)
        rows = pl.ds(r0, C)
        b_all = jnp.dot(tri, g_ref[rows, :], preferred_element_type=F32,
                        precision=lax.Precision.HIGHEST)
        c_ref[slot] = b_all - jnp.log2(jnp.maximum(k_ref[rows, :], 0.0))

        def head(hh):
            cols = slice(hh * HEAD_W, (hh + 1) * HEAD_W)
            b = b_all[:, cols]
            q = q_ref[rows, cols]
            k = k_ref[rows, cols]
            v = v_ref[rows, cols]
            v_bf = v.astype(BF16)
            st = st_ref[hh]
            inter = lax.dot_general((q * jnp.exp2(b)).astype(BF16), st.astype(BF16),
                                    NT, preferred_element_type=F32)
            b_last = b[C - 1:C, :]
            k_dec = (k * jnp.exp2(b_last - b)).astype(BF16)
            st_ref[hh] = st * jnp.exp2(b_last) + lax.dot_general(
                v_bf, k_dec, TN, preferred_element_type=F32)

            q_near, k_near = [], []
            for blk in range(C // SB):
                lo, mid = blk * SB, blk * SB + SUBLANES
                ref = b[mid - 1:mid, :]
                q_near.append(q[mid:mid + SUBLANES, :] * jnp.exp2(b[mid:mid + SUBLANES, :] - ref))
                k_near.append(k[lo:mid, :] * jnp.exp2(ref - b[lo:mid, :]))
                k_near.append(jnp.zeros((SUBLANES, HEAD_W), F32))
            a_near = lax.dot_general(jnp.concatenate(q_near, axis=0).astype(BF16),
                                     jnp.concatenate(k_near, axis=0).astype(BF16),
                                     NT, preferred_element_type=F32)
            yield
            pieces = []
            for blk in range(C // SB):
                lo = blk * SB
                near = a_near[blk * SUBLANES:(blk + 1) * SUBLANES, :]
                if blk > 0:
                    ref = b[lo - 1:lo, :]
                    q_t = (q[lo:lo + SB, :] * jnp.exp2(b[lo:lo + SB, :] - ref)).astype(BF16)
                    k_t = (k[:lo, :] * jnp.exp2(ref - b[:lo, :])).astype(BF16)
                    k_t = jnp.concatenate([k_t, jnp.zeros((C - lo, HEAD_W), BF16)], axis=0)
                    a_far = lax.dot_general(q_t, k_t, NT, preferred_element_type=F32)
                    halves = [a_far[:SUBLANES, :],
                              jnp.where(a_lane >= lo, near, a_far[SUBLANES:, :])]
                else:
                    halves = [jnp.zeros((SUBLANES, C), F32), near]
                for s in range(SB):
                    half = s // SUBLANES
                    r = lo + half * SUBLANES
                    c_s = c_ref[slot, pl.ds(lo + s, 1), cols]
                    w = q[r:r + SUBLANES, :] * jnp.exp2(b[r:r + SUBLANES, :] - c_s)
                    col = jnp.sum(w, axis=-1, keepdims=True)
                    halves[half] = jnp.where(a_lane == lo + s, col, halves[half])
                for half in range(2):
                    keep = a_lane <= a_row + (lo + half * SUBLANES)
                    pieces.append(jnp.where(keep, halves[half], 0.0))
                yield
            a = jnp.concatenate(pieces, axis=0).astype(BF16)
            o = inter + jnp.dot(a, v_bf, preferred_element_type=F32)
            ms = jnp.mean(o * o, axis=-1, keepdims=True)
            o_ref[rows, cols] = (o * lax.rsqrt(ms + NORM_EPS) * nw
                                 * gate_ref[rows, cols]).astype(BF16)

        return [head(hh) for hh in range(N_HEADS)]

    def chunk_group(cg, carry):
        gens = []
        for j in range(HGRN_GROUP):
            gens += chunk_heads(HGRN_GROUP * cg + j, j)
        _run_interleaved(*gens)
        return carry

    n_groups = HGRN_TM // (HGRN_GROUP * C)
    lax.fori_loop(0, n_groups, chunk_group, 0, unroll=n_groups)


def _hgrn(hq, hk, hgl, hi, hgate, norm_w):
    T = hq.shape[0]
    blk = pl.BlockSpec((HGRN_TM, GROUP_W), lambda i: (i, 0))
    return pl.pallas_call(
        _hgrn_kernel,
        grid=(T // HGRN_TM,),
        in_specs=[blk, blk, blk, blk, blk, pl.BlockSpec((1, HEAD_W), lambda i: (0, 0))],
        out_specs=blk,
        out_shape=jax.ShapeDtypeStruct((T, GROUP_W), BF16),
        scratch_shapes=[pltpu.VMEM((N_HEADS, HEAD_W, HEAD_W), F32),
                        pltpu.VMEM((HGRN_GROUP, HGRN_CHUNK, GROUP_W), F32)],
        compiler_params=pltpu.CompilerParams(
            dimension_semantics=("arbitrary",), vmem_limit_bytes=VMEM_LIMIT),
        name="hgrn2",
    )(hq, hk, hgl, hi, hgate, norm_w)


def _attn_kernel(lq1_ref, lk1_ref, lq2_ref, lk2_ref, q_ref, k_ref, vt_ref, nw_ref,
                 o_ref, s_ref, acc_ref, *, lambda_init):
    n_q = q_ref.shape[0] // ATT_TQ
    n_steps = n_q * (n_q + 1) // 2
    assert n_steps % ATT_STEPS == 0 and ATT_STEPS % 2 == 0 and ATT_TQ == ATT_TK
    n_t = ATT_TQ // ATT_TN
    lam = (jnp.exp(jnp.sum(lq1_ref[...] * lk1_ref[...]))
           - jnp.exp(jnp.sum(lq2_ref[...] * lk2_ref[...])) + lambda_init)
    lane = lax.broadcasted_iota(jnp.int32, (ATT_TQ, HEAD_W), 1)
    neg_inf = jnp.full((1, ATT_TN), -jnp.inf, F32)

    acc_ref[...] = jnp.zeros_like(acc_ref)

    def advance(pos):
        qi, t = pos
        row_end = t == qi
        qi_n = jnp.where(row_end, qi + 1, qi)
        t_n = jnp.where(row_end, 0, t + 1)
        done = qi_n == n_q
        return jnp.where(done, n_q - 1, qi_n), jnp.where(done, n_q - 1, t_n)

    def scores(pos, slot, masked):
        qi, t = pos
        q = q_ref[pl.ds(pl.multiple_of(qi * ATT_TQ, ATT_TQ), ATT_TQ), :]
        kb = k_ref[pl.ds(pl.multiple_of(t * ATT_TK, ATT_TK), ATT_TK), :]
        zero = jnp.zeros_like(q)
        q_maps = (jnp.where(lane < QK_DIM, q, zero), jnp.where(lane >= QK_DIM, q, zero))
        mx = []
        for c in range(2):
            for n in range(n_t):
                cols = slice(n * ATT_TN, (n + 1) * ATT_TN)
                s = lax.dot_general(kb, q_maps[c][cols, :], NT,
                                    preferred_element_type=F32)
                if masked:
                    kpos = lax.broadcasted_iota(jnp.int32, s.shape, 0) + t * ATT_TK
                    qpos = lax.broadcasted_iota(jnp.int32, s.shape, 1) + (qi * ATT_TQ + n * ATT_TN)
                    s = jnp.where(kpos <= qpos, s, -jnp.inf)
                s_ref[slot, c, :, cols] = s
                mx.append(jnp.max(s, axis=0, keepdims=True))
                yield
        return tuple(mx)

    def softmax_pv(pos, slot, mx, m_run):
        qi, t = pos
        first = t == 0
        a = qi % ATT_STEPS
        vt = vt_ref[:, pl.ds(pl.multiple_of(t * ATT_TK, ATT_TK), ATT_TK)]
        m_out = []
        for c in range(2):
            for n in range(n_t):
                i = c * n_t + n
                cols = slice(n * ATT_TN, (n + 1) * ATT_TN)
                m_old = jnp.where(first, neg_inf, m_run[i])
                m_new = jnp.maximum(m_old, mx[i])
                alpha = jnp.exp2(m_old - m_new)
                p = jnp.exp2(s_ref[slot, c, :, cols] - m_new).astype(BF16)
                upd = jnp.dot(vt, p, preferred_element_type=F32)
                acc_ref[a, c, :, cols] = alpha * acc_ref[a, c, :, cols] + upd
                m_out.append(m_new)
                yield
        return tuple(m_out)

    def finalize(qi):
        a1 = acc_ref[qi % ATT_STEPS, 0]
        a2 = acc_ref[qi % ATT_STEPS, 1]
        o_t = (a1[:HEAD_W] / a1[HEAD_W:HEAD_W + 1]
               - lam * (a2[:HEAD_W] / a2[HEAD_W:HEAD_W + 1]))
        o = o_t.T
        ms = jnp.mean(o * o, axis=-1, keepdims=True)
        rows = pl.ds(pl.multiple_of(qi * ATT_TQ, ATT_TQ), ATT_TQ)
        o_ref[rows, :] = o * lax.rsqrt(ms + SUBLN_EPS) * nw_ref[...] * (1.0 - lambda_init)

    def on_diag(pos):
        return pos[0] == pos[1]

    def body_block(pos_a, pos_b, mx, m_run, masked_step):
        for j in range(ATT_STEPS):
            mx, m_run = _run_interleaved(scores(pos_a[j], (j + 1) % 2, j == masked_step),
                                         softmax_pv(pos_b[j], j % 2, mx, m_run))
        return mx, m_run

    def positions(pos):
        out = [pos]
        for _ in range(ATT_STEPS - 1):
            out.append(advance(out[-1]))
        return out

    def body(it, carry):
        pos_a0, pos_b0, mx, m_run = carry
        pos_a, pos_b = positions(pos_a0), positions(pos_b0)
        idx = jnp.int32(0)
        for j in range(ATT_STEPS):
            real = it * ATT_STEPS + 1 + j < n_steps
            idx = idx + (j + 1) * (on_diag(pos_a[j]) & real).astype(jnp.int32)
        branches = [functools.partial(lambda ms, _: body_block(pos_a, pos_b, mx, m_run, ms), ms)
                    for ms in [None] + list(range(ATT_STEPS))]
        mx, m_run = lax.switch(idx, branches, 0)

        for j in range(ATT_STEPS):
            @pl.when(on_diag(pos_b[j]))
            def _():
                finalize(pos_b[j][0])

        return advance(pos_a[-1]), advance(pos_b[-1]), mx, m_run

    pos0 = (jnp.int32(0), jnp.int32(0))
    mx0 = _run_interleaved(scores(pos0, 0, True))[0]
    m0 = (neg_inf,) * (2 * n_t)
    lax.fori_loop(0, n_steps // ATT_STEPS, body, (advance(pos0), pos0, mx0, m0))


def _diff_attn(lq1, lk1, lq2, lk2, dq, dk, dvt, norm_w, lambda_init):
    T = dq.shape[0]
    lam_spec = pl.BlockSpec((1, QK_DIM), lambda h: (0, 0))
    return pl.pallas_call(
        functools.partial(_attn_kernel, lambda_init=lambda_init),
        grid=(N_HEADS,),
        in_specs=[
            lam_spec, lam_spec, lam_spec, lam_spec,
            pl.BlockSpec((T, HEAD_W), lambda h: (0, h)),
            pl.BlockSpec((T, HEAD_W), lambda h: (0, h)),
            pl.BlockSpec((None, VT_ROWS, T), lambda h: (h, 0, 0)),
            pl.BlockSpec((1, HEAD_W), lambda h: (0, 0)),
        ],
        out_specs=pl.BlockSpec((T, HEAD_W), lambda h: (0, h)),
        out_shape=jax.ShapeDtypeStruct((T, GROUP_W), F32),
        scratch_shapes=[
            pltpu.VMEM((2, 2, ATT_TK, ATT_TQ), F32),
            pltpu.VMEM((ATT_STEPS, 2, VT_ROWS, ATT_TQ), F32),
        ],
        compiler_params=pltpu.CompilerParams(
            dimension_semantics=("arbitrary",), vmem_limit_bytes=VMEM_LIMIT),
        name="diff_attn",
    )(lq1, lk1, lq2, lk2, dq, dk, dvt, norm_w)


def _out_proj_kernel(ho_ref, do_ref, dgate_ref, x_ref, w_ref, nw_ref, o_ref):
    y = x_ref[...]
    y = y + jnp.dot(ho_ref[...], w_ref[:GROUP_W, :], preferred_element_type=F32)
    mix_d = (do_ref[...] * dgate_ref[...]).astype(BF16)
    y = y + jnp.dot(mix_d, w_ref[GROUP_W:, :], preferred_element_type=F32)
    ms = jnp.mean(y * y, axis=-1, keepdims=True)
    o_ref[...] = y * lax.rsqrt(ms + NORM_EPS) * nw_ref[...]


def _out_proj(ho, do, dgate, x2, w_out_bf, final_norm_w):
    T, D = x2.shape
    row = lambda i: (i, 0)
    fix = lambda i: (0, 0)
    return pl.pallas_call(
        _out_proj_kernel,
        grid=(T // PROJ_TM,),
        in_specs=[
            pl.BlockSpec((PROJ_TM, GROUP_W), row),
            pl.BlockSpec((PROJ_TM, GROUP_W), row),
            pl.BlockSpec((PROJ_TM, GROUP_W), row),
            pl.BlockSpec((PROJ_TM, D), row),
            pl.BlockSpec(w_out_bf.shape, fix),
            pl.BlockSpec((1, D), fix),
        ],
        out_specs=pl.BlockSpec((PROJ_TM, D), row),
        out_shape=jax.ShapeDtypeStruct((T, D), F32),
        compiler_params=pltpu.CompilerParams(
            dimension_semantics=("arbitrary",), vmem_limit_bytes=VMEM_LIMIT),
        name="out_proj",
    )(ho, do, dgate, x2, w_out_bf, final_norm_w)


def _rope_tables(T):
    half = QK_DIM // 2
    inv_freq = 1.0 / (ROPE_THETA ** (jnp.arange(half, dtype=F32) / half))
    ang = inv_freq[:, None] * jnp.arange(T, dtype=F32)[None, :]
    return jnp.cos(ang), jnp.sin(ang)


def kernel(x, norm_w, w_in, hgrn_lb_logits, hgrn_norm_w, diff_lambda_q1, diff_lambda_k1,
           diff_lambda_q2, diff_lambda_k2, diff_norm_w, w_out, final_norm_w):
    B, T, D = x.shape
    depth = norm_w.shape[0]
    assert B == 1 and depth == 1 and D == 2 * GROUP_W
    assert T % ATT_TQ == 0 and T % HGRN_TM == 0 and T % PROJ_TM == 0 and T % IN_TM == 0
    x2 = x.reshape(T, D)
    cos_t, sin_t = _rope_tables(T)
    l = 0
    lambda_init = 0.8 - 0.6 * math.exp(-0.3 * l)
    hq, hk, hgl, hi, hgate, dq, dk, dvt, dgate = _in_proj(
        x2, norm_w[l:l + 1], w_in[l].astype(BF16), hgrn_lb_logits, cos_t, sin_t)
    ho = _hgrn(hq, hk, hgl, hi, hgate, hgrn_norm_w[l:l + 1])
    do = _diff_attn(diff_lambda_q1[l:l + 1], diff_lambda_k1[l:l + 1],
                    diff_lambda_q2[l:l + 1], diff_lambda_k2[l:l + 1],
                    dq, dk, dvt, diff_norm_w[l:l + 1], lambda_init)
    out = _out_proj(ho, do, dgate, x2, w_out[l].astype(BF16), final_norm_w.reshape(1, D))
    return out.reshape(B, T, D)
```

```python
import functools
import math

import jax
import jax.numpy as jnp
from jax import lax
from jax.experimental import pallas as pl
from jax.experimental.pallas import tpu as pltpu

F32 = jnp.float32
BF16 = jnp.bfloat16

SUBLANES = 8
VMEM_LIMIT = 56 * 1024 * 1024

N_HEADS = 4
HEAD_W = 128
GROUP_W = N_HEADS * HEAD_W
QK_DIM = 64
ROPE_THETA = 10000.0
NORM_EPS = 1e-6
SUBLN_EPS = 1e-5

IN_TM = 1024
PROJ_TM = 512
HGRN_CHUNK = 64
HGRN_SUB = 16
HGRN_TM = 512
HGRN_GROUP = 8
ATT_TQ = 512
ATT_TK = 512
ATT_TN = 256
ATT_STEPS = 8
ACC_SLOTS = 4
VT_ROWS = HEAD_W + 16
LOG2E = math.log2(math.e)

NT = (((1,), (1,)), ((), ()))
TN = (((0,), (0,)), ((), ()))


def _run_interleaved(*gens):
    results = [None] * len(gens)
    live = list(range(len(gens)))
    while live:
        for i in list(live):
            try:
                next(gens[i])
            except StopIteration as stop:
                results[i] = stop.value
                live.remove(i)
    return results


def _silu(x):
    return x * (1.0 / (1.0 + jnp.exp(-x)))


def _in_proj_kernel(x_ref, nw_ref, w_ref, lbl_ref, cos_ref, sin_ref,
                    hq_ref, hk_ref, hgl_ref, hi_ref, hgate_ref,
                    dq_ref, dk_ref, dvt_ref, dgate_ref):
    x = x_ref[...]
    ms = jnp.mean(x * x, axis=-1, keepdims=True)
    h = (x * lax.rsqrt(ms + NORM_EPS) * nw_ref[...]).astype(BF16)

    def seg(i):
        return jnp.dot(h, w_ref[:, i * GROUP_W:(i + 1) * GROUP_W],
                       preferred_element_type=F32)

    lg = lbl_ref[...]
    e = jnp.exp(lg - jnp.max(lg, axis=0, keepdims=True))
    lb = e[0:1, :] / jnp.sum(e, axis=0, keepdims=True)

    hq_ref[...] = seg(0)
    f = lb + (1.0 - lb) * (1.0 / (1.0 + jnp.exp(-seg(1))))
    hk_ref[...] = 1.0 - f
    hgl_ref[...] = jnp.log2(f)
    hi_ref[...] = seg(2)
    hgate_ref[...] = _silu(seg(3))

    reps = HEAD_W // (QK_DIM // 2)
    cos = jnp.concatenate([cos_ref[...]] * reps, axis=0).T
    sin_half = sin_ref[...]
    sin = jnp.concatenate([-sin_half, sin_half] * (reps // 2), axis=0).T
    lane = lax.broadcasted_iota(jnp.int32, cos.shape, 1)
    first_half = (lane % QK_DIM) < (QK_DIM // 2)

    def rope(t, scale):
        outs = []
        for hh in range(N_HEADS):
            th = t[:, hh * HEAD_W:(hh + 1) * HEAD_W]
            up = pltpu.roll(th, HEAD_W - QK_DIM // 2, 1)
            dn = pltpu.roll(th, QK_DIM // 2, 1)
            partner = jnp.where(first_half, up, dn)
            outs.append(((th * cos + partner * sin) * scale).astype(BF16))
        return jnp.concatenate(outs, axis=1)

    dq_ref[...] = rope(seg(4), QK_DIM ** -0.5 * LOG2E)
    dk_ref[...] = rope(seg(5), 1.0)
    vt = seg(6).T.astype(BF16)
    for hh in range(N_HEADS):
        dvt_ref[hh, :HEAD_W, :] = vt[hh * HEAD_W:(hh + 1) * HEAD_W, :]
        dvt_ref[hh, HEAD_W:, :] = jnp.ones((VT_ROWS - HEAD_W, IN_TM), BF16)
    dgate_ref[...] = _silu(seg(7))


def _in_proj(x2, norm_w, w_in_bf, lb_logits, cos_t, sin_t):
    T, D = x2.shape
    n = T // IN_TM
    row = lambda i: (i, 0)
    fix = lambda i: (0, 0)
    f32_out = jax.ShapeDtypeStruct((T, GROUP_W), F32)
    bf_out = jax.ShapeDtypeStruct((T, GROUP_W), BF16)
    blk = pl.BlockSpec((IN_TM, GROUP_W), row)
    return pl.pallas_call(
        _in_proj_kernel,
        grid=(n,),
        in_specs=[
            pl.BlockSpec((IN_TM, D), row),
            pl.BlockSpec((1, D), fix),
            pl.BlockSpec(w_in_bf.shape, fix, pipeline_mode=pl.Buffered(1)),
            pl.BlockSpec(lb_logits.shape, fix),
            pl.BlockSpec((QK_DIM // 2, IN_TM), lambda i: (0, i)),
            pl.BlockSpec((QK_DIM // 2, IN_TM), lambda i: (0, i)),
        ],
        out_specs=[blk, blk, blk, blk, blk, blk, blk,
                   pl.BlockSpec((N_HEADS, VT_ROWS, IN_TM), lambda i: (0, 0, i)), blk],
        out_shape=[f32_out, f32_out, f32_out, f32_out, f32_out, bf_out, bf_out,
                   jax.ShapeDtypeStruct((N_HEADS, VT_ROWS, T), BF16), f32_out],
        compiler_params=pltpu.CompilerParams(
            dimension_semantics=("arbitrary",), vmem_limit_bytes=VMEM_LIMIT),
        name="in_proj",
    )(x2, norm_w, w_in_bf, lb_logits, cos_t, sin_t)


def _hgrn_kernel(q_ref, k_ref, g_ref, v_ref, gate_ref, nw_ref, o_ref, st_ref, c_ref):
    C, SB = HGRN_CHUNK, HGRN_SUB

    @pl.when(pl.program_id(0) == 0)
    def _():
        st_ref[...] = jnp.zeros_like(st_ref)

    ti = lax.broadcasted_iota(jnp.int32, (C, C), 0)
    si = lax.broadcasted_iota(jnp.int32, (C, C), 1)
    tri = (si <= ti).astype(F32)
    a_row = lax.broadcasted_iota(jnp.int32, (SUBLANES, C), 0)
    a_lane = lax.broadcasted_iota(jnp.int32, (SUBLANES, C), 1)
    nw = nw_ref[...]

    def chunk_heads(c, slot):
        r0 = pl.multiple_of(c * C, C)
        rows = pl.ds(r0, C)
        b_all = jnp.dot(tri, g_ref[rows, :], preferred_element_type=F32,
                        precision=lax.Precision.HIGHEST)
        c_ref[slot] = b_all - jnp.log2(jnp.maximum(k_ref[rows, :], 0.0))

        def head(hh):
            cols = slice(hh * HEAD_W, (hh + 1) * HEAD_W)
            b = b_all[:, cols]
            q = q_ref[rows, cols]
            k = k_ref[rows, cols]
            v = v_ref[rows, cols]
            v_bf = v.astype(BF16)
            st = st_ref[hh]
            inter = lax.dot_general((q * jnp.exp2(b)).astype(BF16), st.astype(BF16),
                                    NT, preferred_element_type=F32)
            b_last = b[C - 1:C, :]
            k_dec = (k * jnp.exp2(b_last - b)).astype(BF16)
            st_ref[hh] = st * jnp.exp2(b_last) + lax.dot_general(
                v_bf, k_dec, TN, preferred_element_type=F32)

            q_near, k_near = [], []
            for blk in range(C // SB):
                lo, mid = blk * SB, blk * SB + SUBLANES
                ref = b[mid - 1:mid, :]
                q_near.append(q[mid:mid + SUBLANES, :] * jnp.exp2(b[mid:mid + SUBLANES, :] - ref))
                k_near.append(k[lo:mid, :] * jnp.exp2(ref - b[lo:mid, :]))
                k_near.append(jnp.zeros((SUBLANES, HEAD_W), F32))
            a_near = lax.dot_general(jnp.concatenate(q_near, axis=0).astype(BF16),
                                     jnp.concatenate(k_near, axis=0).astype(BF16),
                                     NT, preferred_element_type=F32)
            yield
            pieces = []
            for blk in range(C // SB):
                lo = blk * SB
                near = a_near[blk * SUBLANES:(blk + 1) * SUBLANES, :]
                if blk > 0:
                    ref = b[lo - 1:lo, :]
                    q_t = (q[lo:lo + SB, :] * jnp.exp2(b[lo:lo + SB, :] - ref)).astype(BF16)
                    k_t = (k[:lo, :] * jnp.exp2(ref - b[:lo, :])).astype(BF16)
                    k_t = jnp.concatenate([k_t, jnp.zeros((C - lo, HEAD_W), BF16)], axis=0)
                    a_far = lax.dot_general(q_t, k_t, NT, preferred_element_type=F32)
                    halves = [a_far[:SUBLANES, :],
                              jnp.where(a_lane >= lo, near, a_far[SUBLANES:, :])]
                else:
                    halves = [jnp.zeros((SUBLANES, C), F32), near]
                for s in range(SB):
                    half = s // SUBLANES
                    r = lo + half * SUBLANES
                    c_s = c_ref[slot, pl.ds(lo + s, 1), cols]
                    w = q[r:r + SUBLANES, :] * jnp.exp2(b[r:r + SUBLANES, :] - c_s)
                    col = jnp.sum(w, axis=-1, keepdims=True)
                    halves[half] = jnp.where(a_lane == lo + s, col, halves[half])
                for half in range(2):
                    keep = a_lane <= a_row + (lo + half * SUBLANES)
                    pieces.append(jnp.where(keep, halves[half], 0.0))
                yield
            a = jnp.concatenate(pieces, axis=0).astype(BF16)
            o = inter + jnp.dot(a, v_bf, preferred_element_type=F32)
            ms = jnp.mean(o * o, axis=-1, keepdims=True)
            o_ref[rows, cols] = (o * lax.rsqrt(ms + NORM_EPS) * nw
                                 * gate_ref[rows, cols]).astype(BF16)

        return [head(hh) for hh in range(N_HEADS)]

    def chunk_group(cg, carry):
        gens = []
        for j in range(HGRN_GROUP):
            gens += chunk_heads(HGRN_GROUP * cg + j, j)
        _run_interleaved(*gens)
        return carry

    n_groups = HGRN_TM // (HGRN_GROUP * C)
    lax.fori_loop(0, n_groups, chunk_group, 0, unroll=n_groups)


def _hgrn(hq, hk, hgl, hi, hgate, norm_w):
    T = hq.shape[0]
    blk = pl.BlockSpec((HGRN_TM, GROUP_W), lambda i: (i, 0))
    return pl.pallas_call(
        _hgrn_kernel,
        grid=(T // HGRN_TM,),
        in_specs=[blk, blk, blk, blk, blk, pl.BlockSpec((1, HEAD_W), lambda i: (0, 0))],
        out_specs=blk,
        out_shape=jax.ShapeDtypeStruct((T, GROUP_W), BF16),
        scratch_shapes=[pltpu.VMEM((N_HEADS, HEAD_W, HEAD_W), F32),
                        pltpu.VMEM((HGRN_GROUP, HGRN_CHUNK, GROUP_W), F32)],
        compiler_params=pltpu.CompilerParams(
            dimension_semantics=("arbitrary",), vmem_limit_bytes=VMEM_LIMIT),
        name="hgrn2",
    )(hq, hk, hgl, hi, hgate, norm_w)


def _attn_kernel(lq1_ref, lk1_ref, lq2_ref, lk2_ref, q_ref, k_ref, vt_ref, nw_ref,
                 o_ref, s_ref, acc_ref, *, lambda_init):
    n_q = q_ref.shape[0] // ATT_TQ
    n_steps = n_q * (n_q + 1) // 2
    assert n_steps % ATT_STEPS == 0 and ATT_STEPS % 2 == 0 and ATT_TQ == ATT_TK
    n_t = ATT_TQ // ATT_TN
    lam = (jnp.exp(jnp.sum(lq1_ref[...] * lk1_ref[...]))
           - jnp.exp(jnp.sum(lq2_ref[...] * lk2_ref[...])) + lambda_init)
    lane = lax.broadcasted_iota(jnp.int32, (ATT_TQ, HEAD_W), 1)
    neg_inf = jnp.full((1, ATT_TN), -jnp.inf, F32)

    acc_ref[...] = jnp.zeros_like(acc_ref)

    def advance(pos):
        qi, t = pos
        row_end = t == qi
        qi_n = jnp.where(row_end, qi + 1, qi)
        t_n = jnp.where(row_end, 0, t + 1)
        done = qi_n == n_q
        return jnp.where(done, n_q - 1, qi_n), jnp.where(done, n_q - 1, t_n)

    def scores(pos, slot, masked):
        qi, t = pos
        q = q_ref[pl.ds(pl.multiple_of(qi * ATT_TQ, ATT_TQ), ATT_TQ), :]
        kb = k_ref[pl.ds(pl.multiple_of(t * ATT_TK, ATT_TK), ATT_TK), :]
        zero = jnp.zeros_like(q)
        q_maps = (jnp.where(lane < QK_DIM, q, zero), jnp.where(lane >= QK_DIM, q, zero))
        mx = []
        for c in range(2):
            for n in range(n_t):
                cols = slice(n * ATT_TN, (n + 1) * ATT_TN)
                s = lax.dot_general(kb, q_maps[c][cols, :], NT,
                                    preferred_element_type=F32)
                if masked is not False:
                    kpos = lax.broadcasted_iota(jnp.int32, s.shape, 0) + t * ATT_TK
                    qpos = lax.broadcasted_iota(jnp.int32, s.shape, 1) + (qi * ATT_TQ + n * ATT_TN)
                    keep = kpos <= qpos
                    if masked is not True:
                        keep = keep | jnp.logical_not(masked)
                    s = jnp.where(keep, s, -jnp.inf)
                s_ref[slot, c, :, cols] = s
                mx.append(jnp.max(s, axis=0, keepdims=True))
                yield
        return tuple(mx)

    def softmax_pv(pos, slot, mx, m_run):
        qi, t = pos
        first = t == 0
        a = qi % ACC_SLOTS
        vt = vt_ref[:, pl.ds(pl.multiple_of(t * ATT_TK, ATT_TK), ATT_TK)]
        m_out = []
        for c in range(2):
            for n in range(n_t):
                i = c * n_t + n
                cols = slice(n * ATT_TN, (n + 1) * ATT_TN)
                m_old = jnp.where(first, neg_inf, m_run[i])
                m_new = jnp.maximum(m_old, mx[i])
                alpha = jnp.exp2(m_old - m_new)
                p = jnp.exp2(s_ref[slot, c, :, cols] - m_new).astype(BF16)
                upd = jnp.dot(vt, p, preferred_element_type=F32)
                acc_ref[a, c, :, cols] = alpha * acc_ref[a, c, :, cols] + upd
                m_out.append(m_new)
                yield
        return tuple(m_out)

    def finalize(qi):
        a1 = acc_ref[qi % ACC_SLOTS, 0]
        a2 = acc_ref[qi % ACC_SLOTS, 1]
        o_t = (a1[:HEAD_W] / a1[HEAD_W:HEAD_W + 1]
               - lam * (a2[:HEAD_W] / a2[HEAD_W:HEAD_W + 1]))
        o = o_t.T
        ms = jnp.mean(o * o, axis=-1, keepdims=True)
        rows = pl.ds(pl.multiple_of(qi * ATT_TQ, ATT_TQ), ATT_TQ)
        o_ref[rows, :] = o * lax.rsqrt(ms + SUBLN_EPS) * nw_ref[...] * (1.0 - lambda_init)

    def on_diag(pos):
        return pos[0] == pos[1]

    def body_block(pos_a, pos_b, mx, m_run, masks):
        for j in range(ATT_STEPS):
            mx, m_run = _run_interleaved(scores(pos_a[j], (j + 1) % 2, masks[j]),
                                         softmax_pv(pos_b[j], j % 2, mx, m_run))
        return mx, m_run

    def positions(pos):
        out = [pos]
        for _ in range(ATT_STEPS - 1):
            out.append(advance(out[-1]))
        return out

    def body(it, carry):
        pos_a0, pos_b0, mx, m_run = carry
        pos_a, pos_b = positions(pos_a0), positions(pos_b0)
        flags = [on_diag(pos_a[j]) & (it * ATT_STEPS + 1 + j < n_steps) for j in range(ATT_STEPS)]
        count = sum(f.astype(jnp.int32) for f in flags)
        single = sum((j + 1) * f.astype(jnp.int32) for j, f in enumerate(flags))
        idx = jnp.where(count > 1, ATT_STEPS + 1, single)
        mask_sets = ([[False] * ATT_STEPS]
                     + [[j == k for j in range(ATT_STEPS)] for k in range(ATT_STEPS)]
                     + [flags])
        branches = [functools.partial(lambda ms, _: body_block(pos_a, pos_b, mx, m_run, ms), ms)
                    for ms in mask_sets]
        mx, m_run = lax.switch(idx, branches, 0)

        for j in range(ATT_STEPS):
            @pl.when(on_diag(pos_b[j]))
            def _():
                finalize(pos_b[j][0])

        return advance(pos_a[-1]), advance(pos_b[-1]), mx, m_run

    pos0 = (jnp.int32(0), jnp.int32(0))
    mx0 = _run_interleaved(scores(pos0, 0, True))[0]
    m0 = (neg_inf,) * (2 * n_t)
    lax.fori_loop(0, n_steps // ATT_STEPS, body, (advance(pos0), pos0, mx0, m0))


def _diff_attn(lq1, lk1, lq2, lk2, dq, dk, dvt, norm_w, lambda_init):
    T = dq.shape[0]
    lam_spec = pl.BlockSpec((1, QK_DIM), lambda h: (0, 0))
    return pl.pallas_call(
        functools.partial(_attn_kernel, lambda_init=lambda_init),
        grid=(N_HEADS,),
        in_specs=[
            lam_spec, lam_spec, lam_spec, lam_spec,
            pl.BlockSpec((T, HEAD_W), lambda h: (0, h)),
            pl.BlockSpec((T, HEAD_W), lambda h: (0, h)),
            pl.BlockSpec((None, VT_ROWS, T), lambda h: (h, 0, 0)),
            pl.BlockSpec((1, HEAD_W), lambda h: (0, 0)),
        ],
        out_specs=pl.BlockSpec((T, HEAD_W), lambda h: (0, h)),
        out_shape=jax.ShapeDtypeStruct((T, GROUP_W), F32),
        scratch_shapes=[
            pltpu.VMEM((2, 2, ATT_TK, ATT_TQ), F32),
            pltpu.VMEM((ACC_SLOTS, 2, VT_ROWS, ATT_TQ), F32),
        ],
        compiler_params=pltpu.CompilerParams(
            dimension_semantics=("arbitrary",), vmem_limit_bytes=VMEM_LIMIT),
        name="diff_attn",
    )(lq1, lk1, lq2, lk2, dq, dk, dvt, norm_w)


def _out_proj_kernel(ho_ref, do_ref, dgate_ref, x_ref, w_ref, nw_ref, o_ref):
    y = x_ref[...]
    y = y + jnp.dot(ho_ref[...], w_ref[:GROUP_W, :], preferred_element_type=F32)
    mix_d = (do_ref[...] * dgate_ref[...]).astype(BF16)
    y = y + jnp.dot(mix_d, w_ref[GROUP_W:, :], preferred_element_type=F32)
    ms = jnp.mean(y * y, axis=-1, keepdims=True)
    o_ref[...] = y * lax.rsqrt(ms + NORM_EPS) * nw_ref[...]


def _out_proj(ho, do, dgate, x2, w_out_bf, final_norm_w):
    T, D = x2.shape
    row = lambda i: (i, 0)
    fix = lambda i: (0, 0)
    return pl.pallas_call(
        _out_proj_kernel,
        grid=(T // PROJ_TM,),
        in_specs=[
            pl.BlockSpec((PROJ_TM, GROUP_W), row),
            pl.BlockSpec((PROJ_TM, GROUP_W), row),
            pl.BlockSpec((PROJ_TM, GROUP_W), row),
            pl.BlockSpec((PROJ_TM, D), row),
            pl.BlockSpec(w_out_bf.shape, fix),
            pl.BlockSpec((1, D), fix),
        ],
        out_specs=pl.BlockSpec((PROJ_TM, D), row),
        out_shape=jax.ShapeDtypeStruct((T, D), F32),
        compiler_params=pltpu.CompilerParams(
            dimension_semantics=("arbitrary",), vmem_limit_bytes=VMEM_LIMIT),
        name="out_proj",
    )(ho, do, dgate, x2, w_out_bf, final_norm_w)


def _rope_tables(T):
    half = QK_DIM // 2
    inv_freq = 1.0 / (ROPE_THETA ** (jnp.arange(half, dtype=F32) / half))
    ang = inv_freq[:, None] * jnp.arange(T, dtype=F32)[None, :]
    return jnp.cos(ang), jnp.sin(ang)


def kernel(x, norm_w, w_in, hgrn_lb_logits, hgrn_norm_w, diff_lambda_q1, diff_lambda_k1,
           diff_lambda_q2, diff_lambda_k2, diff_norm_w, w_out, final_norm_w):
    B, T, D = x.shape
    depth = norm_w.shape[0]
    assert B == 1 and depth == 1 and D == 2 * GROUP_W
    assert T % ATT_TQ == 0 and T % HGRN_TM == 0 and T % PROJ_TM == 0 and T % IN_TM == 0
    x2 = x.reshape(T, D)
    cos_t, sin_t = _rope_tables(T)
    l = 0
    lambda_init = 0.8 - 0.6 * math.exp(-0.3 * l)
    hq, hk, hgl, hi, hgate, dq, dk, dvt, dgate = _in_proj(
        x2, norm_w[l:l + 1], w_in[l].astype(BF16), hgrn_lb_logits, cos_t, sin_t)
    ho = _hgrn(hq, hk, hgl, hi, hgate, hgrn_norm_w[l:l + 1])
    do = _diff_attn(diff_lambda_q1[l:l + 1], diff_lambda_k1[l:l + 1],
                    diff_lambda_q2[l:l + 1], diff_lambda_k2[l:l + 1],
                    dq, dk, dvt, diff_norm_w[l:l + 1], lambda_init)
    out = _out_proj(ho, do, dgate, x2, w_out[l].astype(BF16), final_norm_w.reshape(1, D))
    return out.reshape(B, T, D)
```

```python
import functools
import math

import jax
import jax.numpy as jnp
from jax import lax
from jax.experimental import pallas as pl
from jax.experimental.pallas import tpu as pltpu

F32 = jnp.float32
BF16 = jnp.bfloat16

SUBLANES = 8
VMEM_LIMIT = 56 * 1024 * 1024

N_HEADS = 4
HEAD_W = 128
GROUP_W = N_HEADS * HEAD_W
QK_DIM = 64
ROPE_THETA = 10000.0
NORM_EPS = 1e-6
SUBLN_EPS = 1e-5

IN_TM = 1024
PROJ_TM = 1024
HGRN_CHUNK = 64
HGRN_SUB = 16
HGRN_TM = 512
HGRN_GROUP = 8
ATT_TQ = 512
ATT_TK = 512
ATT_TN = 256
ATT_STEPS = 6
ACC_SLOTS = 4
BF16_ROWS = 16
VT_ROWS = HEAD_W + BF16_ROWS
LOG2E = math.log2(math.e)

NT = (((1,), (1,)), ((), ()))
TN = (((0,), (0,)), ((), ()))


def _run_interleaved(*gens):
    results = [None] * len(gens)
    live = list(range(len(gens)))
    while live:
        for i in list(live):
            try:
                next(gens[i])
            except StopIteration as stop:
                results[i] = stop.value
                live.remove(i)
    return results


def _silu(x):
    return x * (1.0 / (1.0 + jnp.exp(-x)))


def _in_proj_kernel(x_ref, nw_ref, w_ref, lbl_ref, cos_ref, sin_ref,
                    hq_ref, hk_ref, hgl_ref, hi_ref, hgate_ref,
                    dq_ref, dk_ref, dvt_ref, dgate_ref):
    x = x_ref[...]
    ms = jnp.mean(x * x, axis=-1, keepdims=True)
    h = (x * lax.rsqrt(ms + NORM_EPS) * nw_ref[...]).astype(BF16)

    def seg(i):
        return jnp.dot(h, w_ref[:, i * GROUP_W:(i + 1) * GROUP_W],
                       preferred_element_type=F32)

    lg = lbl_ref[...]
    e = jnp.exp(lg - jnp.max(lg, axis=0, keepdims=True))
    lb = e[0:1, :] / jnp.sum(e, axis=0, keepdims=True)

    hq_ref[...] = seg(0)
    f = lb + (1.0 - lb) * (1.0 / (1.0 + jnp.exp(-seg(1))))
    hk_ref[...] = 1.0 - f
    hgl_ref[...] = jnp.log2(f)
    hi_ref[...] = seg(2).astype(BF16)
    hgate_ref[...] = _silu(seg(3))

    reps = HEAD_W // (QK_DIM // 2)
    cos = jnp.concatenate([cos_ref[...]] * reps, axis=0).T
    sin_half = sin_ref[...]
    sin = jnp.concatenate([-sin_half, sin_half] * (reps // 2), axis=0).T
    lane = lax.broadcasted_iota(jnp.int32, cos.shape, 1)
    first_half = (lane % QK_DIM) < (QK_DIM // 2)

    def rope(t, scale):
        outs = []
        for hh in range(N_HEADS):
            th = t[:, hh * HEAD_W:(hh + 1) * HEAD_W]
            up = pltpu.roll(th, HEAD_W - QK_DIM // 2, 1)
            dn = pltpu.roll(th, QK_DIM // 2, 1)
            partner = jnp.where(first_half, up, dn)
            outs.append(((th * cos + partner * sin) * scale).astype(BF16))
        return jnp.concatenate(outs, axis=1)

    dq_ref[...] = rope(seg(4), QK_DIM ** -0.5 * LOG2E)
    dk_ref[...] = rope(seg(5), 1.0)
    vt = seg(6).T.astype(BF16)
    for hh in range(N_HEADS):
        dvt_ref[hh, :HEAD_W, :] = vt[hh * HEAD_W:(hh + 1) * HEAD_W, :]
        dvt_ref[hh, HEAD_W:, :] = jnp.ones((VT_ROWS - HEAD_W, IN_TM), BF16)
    dgate_ref[...] = _silu(seg(7))


def _in_proj(x2, norm_w, w_in_bf, lb_logits, cos_t, sin_t):
    T, D = x2.shape
    n = T // IN_TM
    row = lambda i: (i, 0)
    fix = lambda i: (0, 0)
    f32_out = jax.ShapeDtypeStruct((T, GROUP_W), F32)
    bf_out = jax.ShapeDtypeStruct((T, GROUP_W), BF16)
    blk = pl.BlockSpec((IN_TM, GROUP_W), row)
    return pl.pallas_call(
        _in_proj_kernel,
        grid=(n,),
        in_specs=[
            pl.BlockSpec((IN_TM, D), row),
            pl.BlockSpec((1, D), fix),
            pl.BlockSpec(w_in_bf.shape, fix, pipeline_mode=pl.Buffered(1)),
            pl.BlockSpec(lb_logits.shape, fix),
            pl.BlockSpec((QK_DIM // 2, IN_TM), lambda i: (0, i)),
            pl.BlockSpec((QK_DIM // 2, IN_TM), lambda i: (0, i)),
        ],
        out_specs=[blk, blk, blk, blk, blk, blk, blk,
                   pl.BlockSpec((N_HEADS, VT_ROWS, IN_TM), lambda i: (0, 0, i)), blk],
        out_shape=[f32_out, f32_out, f32_out, bf_out, f32_out, bf_out, bf_out,
                   jax.ShapeDtypeStruct((N_HEADS, VT_ROWS, T), BF16), f32_out],
        compiler_params=pltpu.CompilerParams(
            dimension_semantics=("arbitrary",), vmem_limit_bytes=VMEM_LIMIT),
        name="in_proj",
    )(x2, norm_w, w_in_bf, lb_logits, cos_t, sin_t)


def _hgrn_kernel(q_ref, k_ref, g_ref, v_ref, gate_ref, nw_ref, o_ref, st_ref, c_ref):
    C, SB = HGRN_CHUNK, HGRN_SUB

    @pl.when(pl.program_id(0) == 0)
    def _():
        st_ref[...] = jnp.zeros_like(st_ref)

    ti = lax.broadcasted_iota(jnp.int32, (C, C), 0)
    si = lax.broadcasted_iota(jnp.int32, (C, C), 1)
    tri = (si <= ti).astype(F32)
    a_row = lax.broadcasted_iota(jnp.int32, (SUBLANES, C), 0)
    a_lane = lax.broadcasted_iota(jnp.int32, (SUBLANES, C), 1)
    nw = nw_ref[...]

    def chunk_heads(c, slot):
        r0 = pl.multiple_of(c * C, C)
        rows = pl.ds(r0, C)
        b_all = jnp.dot(tri, g_ref[rows, :], preferred_element_type=F32,
                        precision=lax.Precision.HIGHEST)
        c_ref[slot] = b_all - jnp.log2(jnp.maximum(k_ref[rows, :], 0.0))

        def head(hh):
            cols = slice(hh * HEAD_W, (hh + 1) * HEAD_W)
            b = b_all[:, cols]
            q = q_ref[rows, cols]
            k = k_ref[rows, cols]
            v_bf = v_ref[rows, cols]
            st = st_ref[hh]
            inter = lax.dot_general((q * jnp.exp2(b)).astype(BF16), st.astype(BF16),
                                    NT, preferred_element_type=F32)
            b_last = b[C - 1:C, :]
            k_dec = (k * jnp.exp2(b_last - b)).astype(BF16)
            st_ref[hh] = st * jnp.exp2(b_last) + lax.dot_general(
                v_bf, k_dec, TN, preferred_element_type=F32)

            q_near, k_near = [], []
            for blk in range(C // SB):
                lo, mid = blk * SB, blk * SB + SUBLANES
                ref = b[mid - 1:mid, :]
                q_near.append(q[mid:mid + SUBLANES, :] * jnp.exp2(b[mid:mid + SUBLANES, :] - ref))
                k_near.append(k[lo:mid, :] * jnp.exp2(ref - b[lo:mid, :]))
                k_near.append(jnp.zeros((SUBLANES, HEAD_W), F32))
            a_near = lax.dot_general(jnp.concatenate(q_near, axis=0).astype(BF16),
                                     jnp.concatenate(k_near, axis=0).astype(BF16),
                                     NT, preferred_element_type=F32)
            yield
            pieces = []
            for blk in range(C // SB):
                lo = blk * SB
                near = a_near[blk * SUBLANES:(blk + 1) * SUBLANES, :]
                if blk > 0:
                    ref = b[lo - 1:lo, :]
                    q_t = (q[lo:lo + SB, :] * jnp.exp2(b[lo:lo + SB, :] - ref)).astype(BF16)
                    k_t = (k[:lo, :] * jnp.exp2(ref - b[:lo, :])).astype(BF16)
                    k_t = jnp.concatenate([k_t, jnp.zeros((C - lo, HEAD_W), BF16)], axis=0)
                    a_far = lax.dot_general(q_t, k_t, NT, preferred_element_type=F32)
                    halves = [a_far[:SUBLANES, :],
                              jnp.where(a_lane >= lo, near, a_far[SUBLANES:, :])]
                else:
                    halves = [jnp.zeros((SUBLANES, C), F32), near]
                for s in range(SB):
                    half = s // SUBLANES
                    r = lo + half * SUBLANES
                    c_s = c_ref[slot, pl.ds(lo + s, 1), cols]
                    w = q[r:r + SUBLANES, :] * jnp.exp2(b[r:r + SUBLANES, :] - c_s)
                    col = jnp.sum(w, axis=-1, keepdims=True)
                    halves[half] = jnp.where(a_lane == lo + s, col, halves[half])
                for half in range(2):
                    keep = a_lane <= a_row + (lo + half * SUBLANES)
                    pieces.append(jnp.where(keep, halves[half], 0.0))
                yield
            a = jnp.concatenate(pieces, axis=0).astype(BF16)
            o = inter + jnp.dot(a, v_bf, preferred_element_type=F32)
            ms = jnp.mean(o * o, axis=-1, keepdims=True)
            o_ref[rows, cols] = (o * lax.rsqrt(ms + NORM_EPS) * nw
                                 * gate_ref[rows, cols]).astype(BF16)

        return [head(hh) for hh in range(N_HEADS)]

    def chunk_group(cg, carry):
        gens = []
        for j in range(HGRN_GROUP):
            gens += chunk_heads(HGRN_GROUP * cg + j, j)
        _run_interleaved(*gens)
        return carry

    n_groups = HGRN_TM // (HGRN_GROUP * C)
    lax.fori_loop(0, n_groups, chunk_group, 0, unroll=n_groups)


def _hgrn(hq, hk, hgl, hi, hgate, norm_w):
    T = hq.shape[0]
    blk = pl.BlockSpec((HGRN_TM, GROUP_W), lambda i: (i, 0))
    return pl.pallas_call(
        _hgrn_kernel,
        grid=(T // HGRN_TM,),
        in_specs=[blk, blk, blk, blk, blk, pl.BlockSpec((1, HEAD_W), lambda i: (0, 0))],
        out_specs=blk,
        out_shape=jax.ShapeDtypeStruct((T, GROUP_W), BF16),
        scratch_shapes=[pltpu.VMEM((N_HEADS, HEAD_W, HEAD_W), F32),
                        pltpu.VMEM((HGRN_GROUP, HGRN_CHUNK, GROUP_W), F32)],
        compiler_params=pltpu.CompilerParams(
            dimension_semantics=("arbitrary",), vmem_limit_bytes=VMEM_LIMIT),
        name="hgrn2",
    )(hq, hk, hgl, hi, hgate, norm_w)


def _attn_kernel(lq1_ref, lk1_ref, lq2_ref, lk2_ref, q_ref, k_ref, vt_ref, nw_ref,
                 o_ref, s_ref, acc_ref, *, lambda_init):
    n_q = q_ref.shape[0] // ATT_TQ
    n_steps = n_q * (n_q + 1) // 2
    assert n_steps % ATT_STEPS == 0 and ATT_STEPS % 2 == 0 and ATT_TQ == ATT_TK
    n_t = ATT_TQ // ATT_TN
    lam = (jnp.exp(jnp.sum(lq1_ref[...] * lk1_ref[...]))
           - jnp.exp(jnp.sum(lq2_ref[...] * lk2_ref[...])) + lambda_init)
    lane = lax.broadcasted_iota(jnp.int32, (ATT_TQ, HEAD_W), 1)
    neg_inf = jnp.full((1, ATT_TN), -jnp.inf, F32)

    acc_ref[...] = jnp.zeros_like(acc_ref)

    def advance(pos):
        qi, t = pos
        row_end = t == qi
        qi_n = jnp.where(row_end, qi + 1, qi)
        t_n = jnp.where(row_end, 0, t + 1)
        done = qi_n == n_q
        return jnp.where(done, n_q - 1, qi_n), jnp.where(done, n_q - 1, t_n)

    def scores(pos, slot, masked):
        qi, t = pos
        q = q_ref[pl.ds(pl.multiple_of(qi * ATT_TQ, ATT_TQ), ATT_TQ), :]
        kb = k_ref[pl.ds(pl.multiple_of(t * ATT_TK, ATT_TK), ATT_TK), :]
        zero = jnp.zeros_like(q)
        q_maps = (jnp.where(lane < QK_DIM, q, zero), jnp.where(lane >= QK_DIM, q, zero))
        mx = []
        for c in range(2):
            for n in range(n_t):
                cols = slice(n * ATT_TN, (n + 1) * ATT_TN)
                s = lax.dot_general(kb, q_maps[c][cols, :], NT,
                                    preferred_element_type=F32)
                if masked is not False:
                    kpos = lax.broadcasted_iota(jnp.int32, s.shape, 0) + t * ATT_TK
                    qpos = lax.broadcasted_iota(jnp.int32, s.shape, 1) + (qi * ATT_TQ + n * ATT_TN)
                    keep = kpos <= qpos
                    if masked is not True:
                        keep = keep | jnp.logical_not(masked)
                    s = jnp.where(keep, s, -jnp.inf)
                s_ref[slot, c, :, cols] = s
                mx.append(jnp.max(s, axis=0, keepdims=True))
                yield
        return tuple(mx)

    def softmax_pv(pos, slot, mx, m_run):
        qi, t = pos
        first = t == 0
        a = qi % ACC_SLOTS
        vt = vt_ref[:, pl.ds(pl.multiple_of(t * ATT_TK, ATT_TK), ATT_TK)]
        m_out = []
        for c in range(2):
            for n in range(n_t):
                i = c * n_t + n
                cols = slice(n * ATT_TN, (n + 1) * ATT_TN)
                m_old = jnp.where(first, neg_inf, m_run[i])
                m_new = jnp.maximum(m_old, mx[i])
                alpha = jnp.exp2(m_old - m_new)
                p = jnp.exp2(s_ref[slot, c, :, cols] - m_new).astype(BF16)
                upd = jnp.dot(vt, p, preferred_element_type=F32)
                acc_ref[a, c, :, cols] = alpha * acc_ref[a, c, :, cols] + upd
                m_out.append(m_new)
                yield
        return tuple(m_out)

    def finalize(qi):
        a1 = acc_ref[qi % ACC_SLOTS, 0]
        a2 = acc_ref[qi % ACC_SLOTS, 1]
        o_t = (a1[:HEAD_W] / a1[HEAD_W:HEAD_W + 1]
               - lam * (a2[:HEAD_W] / a2[HEAD_W:HEAD_W + 1]))
        o = o_t.T
        ms = jnp.mean(o * o, axis=-1, keepdims=True)
        rows = pl.ds(pl.multiple_of(qi * ATT_TQ, ATT_TQ), ATT_TQ)
        o_ref[rows, :] = o * lax.rsqrt(ms + SUBLN_EPS) * nw_ref[...] * (1.0 - lambda_init)

    def on_diag(pos):
        return pos[0] == pos[1]

    def body_block(pos_a, pos_b, mx, m_run, masks):
        for j in range(ATT_STEPS):
            mx, m_run = _run_interleaved(scores(pos_a[j], (j + 1) % 2, masks[j]),
                                         softmax_pv(pos_b[j], j % 2, mx, m_run))
        return mx, m_run

    def positions(pos):
        out = [pos]
        for _ in range(ATT_STEPS - 1):
            out.append(advance(out[-1]))
        return out

    def body(it, carry):
        pos_a0, pos_b0, mx, m_run = carry
        pos_a, pos_b = positions(pos_a0), positions(pos_b0)
        flags = [on_diag(pos_a[j]) & (it * ATT_STEPS + 1 + j < n_steps) for j in range(ATT_STEPS)]
        count = sum(f.astype(jnp.int32) for f in flags)
        single = sum((j + 1) * f.astype(jnp.int32) for j, f in enumerate(flags))
        idx = jnp.where(count > 1, ATT_STEPS + 1, single)
        mask_sets = ([[False] * ATT_STEPS]
                     + [[j == k for j in range(ATT_STEPS)] for k in range(ATT_STEPS)]
                     + [flags])
        branches = [functools.partial(lambda ms, _: body_block(pos_a, pos_b, mx, m_run, ms), ms)
                    for ms in mask_sets]
        mx, m_run = lax.switch(idx, branches, 0)

        for j in range(ATT_STEPS):
            @pl.when(on_diag(pos_b[j]))
            def _():
                finalize(pos_b[j][0])

        return advance(pos_a[-1]), advance(pos_b[-1]), mx, m_run

    pos0 = (jnp.int32(0), jnp.int32(0))
    mx0 = _run_interleaved(scores(pos0, 0, True))[0]
    m0 = (neg_inf,) * (2 * n_t)
    lax.fori_loop(0, n_steps // ATT_STEPS, body, (advance(pos0), pos0, mx0, m0))


def _diff_attn(lq1, lk1, lq2, lk2, dq, dk, dvt, norm_w, lambda_init):
    T = dq.shape[0]
    lam_spec = pl.BlockSpec((1, QK_DIM), lambda h: (0, 0))
    return pl.pallas_call(
        functools.partial(_attn_kernel, lambda_init=lambda_init),
        grid=(N_HEADS,),
        in_specs=[
            lam_spec, lam_spec, lam_spec, lam_spec,
            pl.BlockSpec((T, HEAD_W), lambda h: (0, h)),
            pl.BlockSpec((T, HEAD_W), lambda h: (0, h)),
            pl.BlockSpec((None, VT_ROWS, T), lambda h: (h, 0, 0)),
            pl.BlockSpec((1, HEAD_W), lambda h: (0, 0)),
        ],
        out_specs=pl.BlockSpec((T, HEAD_W), lambda h: (0, h)),
        out_shape=jax.ShapeDtypeStruct((T, GROUP_W), F32),
        scratch_shapes=[
            pltpu.VMEM((2, 2, ATT_TK, ATT_TQ), F32),
            pltpu.VMEM((ACC_SLOTS, 2, VT_ROWS, ATT_TQ), F32),
        ],
        compiler_params=pltpu.CompilerParams(
            dimension_semantics=("arbitrary",), vmem_limit_bytes=VMEM_LIMIT),
        name="diff_attn",
    )(lq1, lk1, lq2, lk2, dq, dk, dvt, norm_w)


def _out_proj_kernel(ho_ref, do_ref, dgate_ref, x_ref, w_ref, nw_ref, o_ref):
    y = x_ref[...]
    y = y + jnp.dot(ho_ref[...], w_ref[:GROUP_W, :], preferred_element_type=F32)
    mix_d = (do_ref[...] * dgate_ref[...]).astype(BF16)
    y = y + jnp.dot(mix_d, w_ref[GROUP_W:, :], preferred_element_type=F32)
    ms = jnp.mean(y * y, axis=-1, keepdims=True)
    o_ref[...] = y * lax.rsqrt(ms + NORM_EPS) * nw_ref[...]


def _out_proj(ho, do, dgate, x2, w_out_bf, final_norm_w):
    T, D = x2.shape
    row = lambda i: (i, 0)
    fix = lambda i: (0, 0)
    return pl.pallas_call(
        _out_proj_kernel,
        grid=(T // PROJ_TM,),
        in_specs=[
            pl.BlockSpec((PROJ_TM, GROUP_W), row),
            pl.BlockSpec((PROJ_TM, GROUP_W), row),
            pl.BlockSpec((PROJ_TM, GROUP_W), row),
            pl.BlockSpec((PROJ_TM, D), row),
            pl.BlockSpec(w_out_bf.shape, fix),
            pl.BlockSpec((1, D), fix),
        ],
        out_specs=pl.BlockSpec((PROJ_TM, D), row),
        out_shape=jax.ShapeDtypeStruct((T, D), F32),
        compiler_params=pltpu.CompilerParams(
            dimension_semantics=("arbitrary",), vmem_limit_bytes=VMEM_LIMIT),
        name="out_proj",
    )(ho, do, dgate, x2, w_out_bf, final_norm_w)


def _rope_tables(T):
    half = QK_DIM // 2
    inv_freq = 1.0 / (ROPE_THETA ** (jnp.arange(half, dtype=F32) / half))
    ang = inv_freq[:, None] * jnp.arange(T, dtype=F32)[None, :]
    return jnp.cos(ang), jnp.sin(ang)


def kernel(x, norm_w, w_in, hgrn_lb_logits, hgrn_norm_w, diff_lambda_q1, diff_lambda_k1,
           diff_lambda_q2, diff_lambda_k2, diff_norm_w, w_out, final_norm_w):
    B, T, D = x.shape
    depth = norm_w.shape[0]
    assert B == 1 and depth == 1 and D == 2 * GROUP_W
    assert T % ATT_TQ == 0 and T % HGRN_TM == 0 and T % PROJ_TM == 0 and T % IN_TM == 0
    x2 = x.reshape(T, D)
    cos_t, sin_t = _rope_tables(T)
    l = 0
    lambda_init = 0.8 - 0.6 * math.exp(-0.3 * l)
    hq, hk, hgl, hi, hgate, dq, dk, dvt, dgate = _in_proj(
        x2, norm_w[l:l + 1], w_in[l].astype(BF16), hgrn_lb_logits, cos_t, sin_t)
    ho = _hgrn(hq, hk, hgl, hi, hgate, hgrn_norm_w[l:l + 1])
    do = _diff_attn(diff_lambda_q1[l:l + 1], diff_lambda_k1[l:l + 1],
                    diff_lambda_q2[l:l + 1], diff_lambda_k2[l:l + 1],
                    dq, dk, dvt, diff_norm_w[l:l + 1], lambda_init)
    out = _out_proj(ho, do, dgate, x2, w_out[l].astype(BF16), final_norm_w.reshape(1, D))
    return out.reshape(B, T, D)
```

```python
import functools
import math

import jax
import jax.numpy as jnp
from jax import lax
from jax.experimental import pallas as pl
from jax.experimental.pallas import tpu as pltpu

F32 = jnp.float32
BF16 = jnp.bfloat16

SUBLANES = 8
VMEM_LIMIT = 56 * 1024 * 1024

N_HEADS = 4
HEAD_W = 128
GROUP_W = N_HEADS * HEAD_W
QK_DIM = 64
ROPE_THETA = 10000.0
NORM_EPS = 1e-6
SUBLN_EPS = 1e-5

IN_TM = 1024
PROJ_TM = 1024
HGRN_CHUNK = 64
HGRN_SUB = 16
HGRN_TM = 512
HGRN_GROUP = 8
ATT_TQ = 512
ATT_TK = 512
ATT_TN = 256
ATT_STEPS = 6
ACC_SLOTS = 4
BF16_ROWS = 16
VT_ROWS = HEAD_W + BF16_ROWS
LOG2E = math.log2(math.e)

NT = (((1,), (1,)), ((), ()))
TN = (((0,), (0,)), ((), ()))


def _run_interleaved(*gens):
    results = [None] * len(gens)
    live = list(range(len(gens)))
    while live:
        for i in list(live):
            try:
                next(gens[i])
            except StopIteration as stop:
                results[i] = stop.value
                live.remove(i)
    return results


def _silu(x):
    return x * (1.0 / (1.0 + jnp.exp(-x)))


def _in_proj_kernel(x_ref, nw_ref, w_ref, lbl_ref, cos_ref, sin_ref,
                    hq_ref, hk_ref, hgl_ref, hi_ref, hgate_ref,
                    dq_ref, dk_ref, dvt_ref, dgate_ref):
    x = x_ref[...]
    ms = jnp.mean(x * x, axis=-1, keepdims=True)
    h = (x * lax.rsqrt(ms + NORM_EPS) * nw_ref[...]).astype(BF16)

    def seg(i):
        return jnp.dot(h, w_ref[:, i * GROUP_W:(i + 1) * GROUP_W],
                       preferred_element_type=F32)

    lg = lbl_ref[...]
    e = jnp.exp(lg - jnp.max(lg, axis=0, keepdims=True))
    lb = e[0:1, :] / jnp.sum(e, axis=0, keepdims=True)

    hq_ref[...] = seg(0)
    f = lb + (1.0 - lb) * (1.0 / (1.0 + jnp.exp(-seg(1))))
    hk_ref[...] = 1.0 - f
    hgl_ref[...] = jnp.log2(f)
    hi_ref[...] = seg(2).astype(BF16)
    hgate_ref[...] = _silu(seg(3))

    reps = HEAD_W // (QK_DIM // 2)
    cos = jnp.concatenate([cos_ref[...]] * reps, axis=0).T
    sin_half = sin_ref[...]
    sin = jnp.concatenate([-sin_half, sin_half] * (reps // 2), axis=0).T
    lane = lax.broadcasted_iota(jnp.int32, cos.shape, 1)
    first_half = (lane % QK_DIM) < (QK_DIM // 2)

    def rope(t, scale):
        outs = []
        for hh in range(N_HEADS):
            th = t[:, hh * HEAD_W:(hh + 1) * HEAD_W]
            up = pltpu.roll(th, HEAD_W - QK_DIM // 2, 1)
            dn = pltpu.roll(th, QK_DIM // 2, 1)
            partner = jnp.where(first_half, up, dn)
            outs.append(((th * cos + partner * sin) * scale).astype(BF16))
        return jnp.concatenate(outs, axis=1)

    dq_ref[...] = rope(seg(4), QK_DIM ** -0.5 * LOG2E)
    dk_ref[...] = rope(seg(5), 1.0)
    vt = seg(6).T.astype(BF16)
    for hh in range(N_HEADS):
        dvt_ref[hh, :HEAD_W, :] = vt[hh * HEAD_W:(hh + 1) * HEAD_W, :]
        dvt_ref[hh, HEAD_W:, :] = jnp.ones((VT_ROWS - HEAD_W, IN_TM), BF16)
    dgate_ref[...] = _silu(seg(7))


def _in_proj(x2, norm_w, w_in_bf, lb_logits, cos_t, sin_t):
    T, D = x2.shape
    n = T // IN_TM
    row = lambda i: (i, 0)
    fix = lambda i: (0, 0)
    f32_out = jax.ShapeDtypeStruct((T, GROUP_W), F32)
    bf_out = jax.ShapeDtypeStruct((T, GROUP_W), BF16)
    blk = pl.BlockSpec((IN_TM, GROUP_W), row)
    return pl.pallas_call(
        _in_proj_kernel,
        grid=(n,),
        in_specs=[
            pl.BlockSpec((IN_TM, D), row),
            pl.BlockSpec((1, D), fix),
            pl.BlockSpec(w_in_bf.shape, fix, pipeline_mode=pl.Buffered(1)),
            pl.BlockSpec(lb_logits.shape, fix),
            pl.BlockSpec((QK_DIM // 2, IN_TM), lambda i: (0, i)),
            pl.BlockSpec((QK_DIM // 2, IN_TM), lambda i: (0, i)),
        ],
        out_specs=[blk, blk, blk, blk, blk, blk, blk,
                   pl.BlockSpec((N_HEADS, VT_ROWS, IN_TM), lambda i: (0, 0, i)), blk],
        out_shape=[f32_out, f32_out, f32_out, bf_out, f32_out, bf_out, bf_out,
                   jax.ShapeDtypeStruct((N_HEADS, VT_ROWS, T), BF16), f32_out],
        compiler_params=pltpu.CompilerParams(
            dimension_semantics=("arbitrary",), vmem_limit_bytes=VMEM_LIMIT),
        name="in_proj",
    )(x2, norm_w, w_in_bf, lb_logits, cos_t, sin_t)


def _hgrn_kernel(q_ref, k_ref, g_ref, v_ref, gate_ref, nw_ref, o_ref, st_ref, c_ref):
    C, SB = HGRN_CHUNK, HGRN_SUB

    @pl.when(pl.program_id(0) == 0)
    def _():
        st_ref[...] = jnp.zeros_like(st_ref)

    ti = lax.broadcasted_iota(jnp.int32, (C, C), 0)
    si = lax.broadcasted_iota(jnp.int32, (C, C), 1)
    tri = (si <= ti).astype(F32)
    a_row = lax.broadcasted_iota(jnp.int32, (SUBLANES, C), 0)
    a_lane = lax.broadcasted_iota(jnp.int32, (SUBLANES, C), 1)
    nw = nw_ref[...]

    def chunk_heads(c, slot):
        r0 = pl.multiple_of(c * C, C)
        rows = pl.ds(r0, C)
        b_all = jnp.dot(tri, g_ref[rows, :], preferred_element_type=F32,
                        precision=lax.Precision.HIGHEST)
        c_ref[slot] = b_all - jnp.log2(jnp.maximum(k_ref[rows, :], 0.0))

        def head(hh):
            cols = slice(hh * HEAD_W, (hh + 1) * HEAD_W)
            b = b_all[:, cols]
            q = q_ref[rows, cols]
            k = k_ref[rows, cols]
            v_bf = v_ref[rows, cols]
            st = st_ref[hh]
            inter = lax.dot_general((q * jnp.exp2(b)).astype(BF16), st.astype(BF16),
                                    NT, preferred_element_type=F32)
            b_last = b[C - 1:C, :]
            k_dec = (k * jnp.exp2(b_last - b)).astype(BF16)
            st_ref[hh] = st * jnp.exp2(b_last) + lax.dot_general(
                v_bf, k_dec, TN, preferred_element_type=F32)

            q_near, k_near = [], []
            for blk in range(C // SB):
                lo, mid = blk * SB, blk * SB + SUBLANES
                ref = b[mid - 1:mid, :]
                q_near.append(q[mid:mid + SUBLANES, :] * jnp.exp2(b[mid:mid + SUBLANES, :] - ref))
                k_near.append(k[lo:mid, :] * jnp.exp2(ref - b[lo:mid, :]))
                k_near.append(jnp.zeros((SUBLANES, HEAD_W), F32))
            a_near = lax.dot_general(jnp.concatenate(q_near, axis=0).astype(BF16),
                                     jnp.concatenate(k_near, axis=0).astype(BF16),
                                     NT, preferred_element_type=F32)
            yield
            pieces = []
            for blk in range(C // SB):
                lo = blk * SB
                near = a_near[blk * SUBLANES:(blk + 1) * SUBLANES, :]
                if blk > 0:
                    ref = b[lo - 1:lo, :]
                    q_t = (q[lo:lo + SB, :] * jnp.exp2(b[lo:lo + SB, :] - ref)).astype(BF16)
                    k_t = (k[:lo, :] * jnp.exp2(ref - b[:lo, :])).astype(BF16)
                    k_t = jnp.concatenate([k_t, jnp.zeros((C - lo, HEAD_W), BF16)], axis=0)
                    a_far = lax.dot_general(q_t, k_t, NT, preferred_element_type=F32)
                    halves = [a_far[:SUBLANES, :],
                              jnp.where(a_lane >= lo, near, a_far[SUBLANES:, :])]
                else:
                    halves = [jnp.zeros((SUBLANES, C), F32), near]
                for s in range(SB):
                    half = s // SUBLANES
                    r = lo + half * SUBLANES
                    c_s = c_ref[slot, pl.ds(lo + s, 1), cols]
                    w = q[r:r + SUBLANES, :] * jnp.exp2(b[r:r + SUBLANES, :] - c_s)
                    col = jnp.sum(w, axis=-1, keepdims=True)
                    halves[half] = jnp.where(a_lane == lo + s, col, halves[half])
                for half in range(2):
                    keep = a_lane <= a_row + (lo + half * SUBLANES)
                    pieces.append(jnp.where(keep, halves[half], 0.0))
                yield
            a = jnp.concatenate(pieces, axis=0).astype(BF16)
            o = inter + jnp.dot(a, v_bf, preferred_element_type=F32)
            ms = jnp.mean(o * o, axis=-1, keepdims=True)
            o_ref[rows, cols] = (o * lax.rsqrt(ms + NORM_EPS) * nw
                                 * gate_ref[rows, cols]).astype(BF16)

        return [head(hh) for hh in range(N_HEADS)]

    def chunk_group(cg, carry):
        gens = []
        for j in range(HGRN_GROUP):
            gens += chunk_heads(HGRN_GROUP * cg + j, j)
        _run_interleaved(*gens)
        return carry

    n_groups = HGRN_TM // (HGRN_GROUP * C)
    lax.fori_loop(0, n_groups, chunk_group, 0, unroll=n_groups)


def _hgrn(hq, hk, hgl, hi, hgate, norm_w):
    T = hq.shape[0]
    blk = pl.BlockSpec((HGRN_TM, GROUP_W), lambda i: (i, 0))
    return pl.pallas_call(
        _hgrn_kernel,
        grid=(T // HGRN_TM,),
        in_specs=[blk, blk, blk, blk, blk, pl.BlockSpec((1, HEAD_W), lambda i: (0, 0))],
        out_specs=blk,
        out_shape=jax.ShapeDtypeStruct((T, GROUP_W), BF16),
        scratch_shapes=[pltpu.VMEM((N_HEADS, HEAD_W, HEAD_W), F32),
                        pltpu.VMEM((HGRN_GROUP, HGRN_CHUNK, GROUP_W), F32)],
        compiler_params=pltpu.CompilerParams(
            dimension_semantics=("arbitrary",), vmem_limit_bytes=VMEM_LIMIT),
        name="hgrn2",
    )(hq, hk, hgl, hi, hgate, norm_w)


def _attn_kernel(lq1_ref, lk1_ref, lq2_ref, lk2_ref, q_ref, k_ref, vt_ref, nw_ref,
                 o_ref, s_ref, acc_ref, *, lambda_init):
    n_q = q_ref.shape[0] // ATT_TQ
    n_steps = n_q * (n_q + 1) // 2
    assert n_steps % ATT_STEPS == 0 and ATT_STEPS % 2 == 0 and ATT_TQ == ATT_TK
    n_t = ATT_TQ // ATT_TN
    lam = (jnp.exp(jnp.sum(lq1_ref[...] * lk1_ref[...]))
           - jnp.exp(jnp.sum(lq2_ref[...] * lk2_ref[...])) + lambda_init)
    lane = lax.broadcasted_iota(jnp.int32, (ATT_TQ, HEAD_W), 1)
    neg_inf = jnp.full((1, ATT_TN), -jnp.inf, F32)

    acc_ref[...] = jnp.zeros_like(acc_ref)

    def advance(pos):
        qi, t = pos
        row_end = t == qi
        qi_n = jnp.where(row_end, qi + 1, qi)
        t_n = jnp.where(row_end, 0, t + 1)
        done = qi_n == n_q
        return jnp.where(done, n_q - 1, qi_n), jnp.where(done, n_q - 1, t_n)

    def scores(pos, slot, masked):
        qi, t = pos
        q = q_ref[pl.ds(pl.multiple_of(qi * ATT_TQ, ATT_TQ), ATT_TQ), :]
        kb = k_ref[pl.ds(pl.multiple_of(t * ATT_TK, ATT_TK), ATT_TK), :]
        zero = jnp.zeros_like(q)
        q_maps = (jnp.where(lane < QK_DIM, q, zero), jnp.where(lane >= QK_DIM, q, zero))
        mx = []
        for c in range(2):
            for n in range(n_t):
                cols = slice(n * ATT_TN, (n + 1) * ATT_TN)
                s = lax.dot_general(kb, q_maps[c][cols, :], NT,
                                    preferred_element_type=F32)
                if masked is not False:
                    kpos = lax.broadcasted_iota(jnp.int32, s.shape, 0) + t * ATT_TK
                    qpos = lax.broadcasted_iota(jnp.int32, s.shape, 1) + (qi * ATT_TQ + n * ATT_TN)
                    keep = kpos <= qpos
                    if masked is not True:
                        keep = keep | jnp.logical_not(masked)
                    s = jnp.where(keep, s, -jnp.inf)
                s_ref[slot, c, :, cols] = s
                mx.append(jnp.max(s, axis=0, keepdims=True))
                yield
        return tuple(mx)

    def softmax_pv(pos, slot, mx, m_run):
        qi, t = pos
        first = t == 0
        a = qi % ACC_SLOTS
        vt = vt_ref[:, pl.ds(pl.multiple_of(t * ATT_TK, ATT_TK), ATT_TK)]
        m_out = []
        for c in range(2):
            for n in range(n_t):
                i = c * n_t + n
                cols = slice(n * ATT_TN, (n + 1) * ATT_TN)
                m_old = jnp.where(first, neg_inf, m_run[i])
                m_new = jnp.maximum(m_old, mx[i])
                alpha = jnp.exp2(m_old - m_new)
                p = jnp.exp2(s_ref[slot, c, :, cols] - m_new).astype(BF16)
                upd = jnp.dot(vt, p, preferred_element_type=F32)
                acc_ref[a, c, :, cols] = alpha * acc_ref[a, c, :, cols] + upd
                m_out.append(m_new)
                yield
        return tuple(m_out)

    def finalize(qi):
        a1 = acc_ref[qi % ACC_SLOTS, 0]
        a2 = acc_ref[qi % ACC_SLOTS, 1]
        o_t = (a1[:HEAD_W] / a1[HEAD_W:HEAD_W + 1]
               - lam * (a2[:HEAD_W] / a2[HEAD_W:HEAD_W + 1]))
        o = o_t.T
        ms = jnp.mean(o * o, axis=-1, keepdims=True)
        rows = pl.ds(pl.multiple_of(qi * ATT_TQ, ATT_TQ), ATT_TQ)
        o_ref[rows, :] = o * lax.rsqrt(ms + SUBLN_EPS) * nw_ref[...] * (1.0 - lambda_init)

    def on_diag(pos):
        return pos[0] == pos[1]

    def body_block(pos_a, pos_b, mx, m_run, masks, fin_step):
        for j in range(ATT_STEPS):
            mx, m_run = _run_interleaved(scores(pos_a[j], (j + 1) % 2, masks[j]),
                                         softmax_pv(pos_b[j], j % 2, mx, m_run))
            if j == fin_step:
                finalize(pos_b[j][0])
        return mx, m_run

    def positions(pos):
        out = [pos]
        for _ in range(ATT_STEPS - 1):
            out.append(advance(out[-1]))
        return out

    def body(it, carry):
        pos_a0, pos_b0, mx, m_run = carry
        pos_a, pos_b = positions(pos_a0), positions(pos_b0)
        flags = [on_diag(pos_a[j]) & (it * ATT_STEPS + 1 + j < n_steps) for j in range(ATT_STEPS)]
        count = sum(f.astype(jnp.int32) for f in flags)
        single = sum((j + 1) * f.astype(jnp.int32) for j, f in enumerate(flags))
        idx = jnp.where(count > 1, ATT_STEPS + 1, single)
        mask_sets = ([[False] * ATT_STEPS]
                     + [[j == k for j in range(ATT_STEPS)] for k in range(ATT_STEPS)]
                     + [flags])
        fin_steps = ([None] + [k + 1 if k + 1 < ATT_STEPS else None for k in range(ATT_STEPS)]
                     + [None])
        branches = [functools.partial(
            lambda ms, fs, _: body_block(pos_a, pos_b, mx, m_run, ms, fs), ms, fs)
            for ms, fs in zip(mask_sets, fin_steps)]
        mx, m_run = lax.switch(idx, branches, 0)

        in_line = jnp.where((count == 1) & (single < ATT_STEPS), single, -1)
        for j in range(ATT_STEPS):
            @pl.when(on_diag(pos_b[j]) & (in_line != j))
            def _():
                finalize(pos_b[j][0])

        return advance(pos_a[-1]), advance(pos_b[-1]), mx, m_run

    pos0 = (jnp.int32(0), jnp.int32(0))
    mx0 = _run_interleaved(scores(pos0, 0, True))[0]
    m0 = (neg_inf,) * (2 * n_t)
    lax.fori_loop(0, n_steps // ATT_STEPS, body, (advance(pos0), pos0, mx0, m0))


def _diff_attn(lq1, lk1, lq2, lk2, dq, dk, dvt, norm_w, lambda_init):
    T = dq.shape[0]
    lam_spec = pl.BlockSpec((1, QK_DIM), lambda h: (0, 0))
    return pl.pallas_call(
        functools.partial(_attn_kernel, lambda_init=lambda_init),
        grid=(N_HEADS,),
        in_specs=[
            lam_spec, lam_spec, lam_spec, lam_spec,
            pl.BlockSpec((T, HEAD_W), lambda h: (0, h)),
            pl.BlockSpec((T, HEAD_W), lambda h: (0, h)),
            pl.BlockSpec((None, VT_ROWS, T), lambda h: (h, 0, 0)),
            pl.BlockSpec((1, HEAD_W), lambda h: (0, 0)),
        ],
        out_specs=pl.BlockSpec((T, HEAD_W), lambda h: (0, h)),
        out_shape=jax.ShapeDtypeStruct((T, GROUP_W), F32),
        scratch_shapes=[
            pltpu.VMEM((2, 2, ATT_TK, ATT_TQ), F32),
            pltpu.VMEM((ACC_SLOTS, 2, VT_ROWS, ATT_TQ), F32),
        ],
        compiler_params=pltpu.CompilerParams(
            dimension_semantics=("arbitrary",), vmem_limit_bytes=VMEM_LIMIT),
        name="diff_attn",
    )(lq1, lk1, lq2, lk2, dq, dk, dvt, norm_w)


def _out_proj_kernel(ho_ref, do_ref, dgate_ref, x_ref, w_ref, nw_ref, o_ref):
    y = x_ref[...]
    y = y + jnp.dot(ho_ref[...], w_ref[:GROUP_W, :], preferred_element_type=F32)
    mix_d = (do_ref[...] * dgate_ref[...]).astype(BF16)
    y = y + jnp.dot(mix_d, w_ref[GROUP_W:, :], preferred_element_type=F32)
    ms = jnp.mean(y * y, axis=-1, keepdims=True)
    o_ref[...] = y * lax.rsqrt(ms + NORM_EPS) * nw_ref[...]


def _out_proj(ho, do, dgate, x2, w_out_bf, final_norm_w):
    T, D = x2.shape
    row = lambda i: (i, 0)
    fix = lambda i: (0, 0)
    return pl.pallas_call(
        _out_proj_kernel,
        grid=(T // PROJ_TM,),
        in_specs=[
            pl.BlockSpec((PROJ_TM, GROUP_W), row),
            pl.BlockSpec((PROJ_TM, GROUP_W), row),
            pl.BlockSpec((PROJ_TM, GROUP_W), row),
            pl.BlockSpec((PROJ_TM, D), row),
            pl.BlockSpec(w_out_bf.shape, fix),
            pl.BlockSpec((1, D), fix),
        ],
        out_specs=pl.BlockSpec((PROJ_TM, D), row),
        out_shape=jax.ShapeDtypeStruct((T, D), F32),
        compiler_params=pltpu.CompilerParams(
            dimension_semantics=("arbitrary",), vmem_limit_bytes=VMEM_LIMIT),
        name="out_proj",
    )(ho, do, dgate, x2, w_out_bf, final_norm_w)


def _rope_tables(T):
    half = QK_DIM // 2
    inv_freq = 1.0 / (ROPE_THETA ** (jnp.arange(half, dtype=F32) / half))
    ang = inv_freq[:, None] * jnp.arange(T, dtype=F32)[None, :]
    return jnp.cos(ang), jnp.sin(ang)


def kernel(x, norm_w, w_in, hgrn_lb_logits, hgrn_norm_w, diff_lambda_q1, diff_lambda_k1,
           diff_lambda_q2, diff_lambda_k2, diff_norm_w, w_out, final_norm_w):
    B, T, D = x.shape
    depth = norm_w.shape[0]
    assert B == 1 and depth == 1 and D == 2 * GROUP_W
    assert T % ATT_TQ == 0 and T % HGRN_TM == 0 and T % PROJ_TM == 0 and T % IN_TM == 0
    x2 = x.reshape(T, D)
    cos_t, sin_t = _rope_tables(T)
    l = 0
    lambda_init = 0.8 - 0.6 * math.exp(-0.3 * l)
    hq, hk, hgl, hi, hgate, dq, dk, dvt, dgate = _in_proj(
        x2, norm_w[l:l + 1], w_in[l].astype(BF16), hgrn_lb_logits, cos_t, sin_t)
    ho = _hgrn(hq, hk, hgl, hi, hgate, hgrn_norm_w[l:l + 1])
    do = _diff_attn(diff_lambda_q1[l:l + 1], diff_lambda_k1[l:l + 1],
                    diff_lambda_q2[l:l + 1], diff_lambda_k2[l:l + 1],
                    dq, dk, dvt, diff_norm_w[l:l + 1], lambda_init)
    out = _out_proj(ho, do, dgate, x2, w_out[l].astype(BF16), final_norm_w.reshape(1, D))
    return out.reshape(B, T, D)
```

```python
import functools
import math

import jax
import jax.numpy as jnp
from jax import lax
from jax.experimental import pallas as pl
from jax.experimental.pallas import tpu as pltpu

F32 = jnp.float32
BF16 = jnp.bfloat16

SUBLANES = 8
VMEM_LIMIT = 56 * 1024 * 1024

N_HEADS = 4
HEAD_W = 128
GROUP_W = N_HEADS * HEAD_W
QK_DIM = 64
ROPE_THETA = 10000.0
NORM_EPS = 1e-6
SUBLN_EPS = 1e-5

IN_TM = 1024
PROJ_TM = 1024
HGRN_CHUNK = 64
HGRN_SUB = 16
HGRN_TM = 512
HGRN_GROUP = 8
ATT_TQ = 512
ATT_TK = 512
ATT_TN = 256
ATT_STEPS = 6
ACC_SLOTS = 4
BF16_ROWS = 16
VT_ROWS = HEAD_W + BF16_ROWS
LOG2E = math.log2(math.e)

NT = (((1,), (1,)), ((), ()))
TN = (((0,), (0,)), ((), ()))


def _run_interleaved(*gens):
    results = [None] * len(gens)
    live = list(range(len(gens)))
    while live:
        for i in list(live):
            try:
                next(gens[i])
            except StopIteration as stop:
                results[i] = stop.value
                live.remove(i)
    return results


def _silu(x):
    return x * (1.0 / (1.0 + jnp.exp(-x)))


def _in_proj_kernel(x_ref, nw_ref, w_ref, lbl_ref, cos_ref, sin_ref,
                    hq_ref, hk_ref, hgl_ref, hi_ref, hgate_ref,
                    dq_ref, dk_ref, dvt_ref, dgate_ref):
    x = x_ref[...]
    ms = jnp.mean(x * x, axis=-1, keepdims=True)
    h = (x * lax.rsqrt(ms + NORM_EPS) * nw_ref[...]).astype(BF16)

    def seg(i):
        return jnp.dot(h, w_ref[:, i * GROUP_W:(i + 1) * GROUP_W],
                       preferred_element_type=F32)

    lg = lbl_ref[...]
    e = jnp.exp(lg - jnp.max(lg, axis=0, keepdims=True))
    lb = e[0:1, :] / jnp.sum(e, axis=0, keepdims=True)

    hq_ref[...] = seg(0)
    f = lb + (1.0 - lb) * (1.0 / (1.0 + jnp.exp(-seg(1))))
    hk_ref[...] = 1.0 - f
    hgl_ref[...] = jnp.log2(f)
    hi_ref[...] = seg(2).astype(BF16)
    hgate_ref[...] = _silu(seg(3))

    reps = HEAD_W // (QK_DIM // 2)
    cos = jnp.concatenate([cos_ref[...]] * reps, axis=0).T
    sin_half = sin_ref[...]
    sin = jnp.concatenate([-sin_half, sin_half] * (reps // 2), axis=0).T
    lane = lax.broadcasted_iota(jnp.int32, cos.shape, 1)
    first_half = (lane % QK_DIM) < (QK_DIM // 2)

    def rope(t, scale):
        outs = []
        for hh in range(N_HEADS):
            th = t[:, hh * HEAD_W:(hh + 1) * HEAD_W]
            up = pltpu.roll(th, HEAD_W - QK_DIM // 2, 1)
            dn = pltpu.roll(th, QK_DIM // 2, 1)
            partner = jnp.where(first_half, up, dn)
            outs.append(((th * cos + partner * sin) * scale).astype(BF16))
        return jnp.concatenate(outs, axis=1)

    dq_ref[...] = rope(seg(4), QK_DIM ** -0.5 * LOG2E)
    dk_ref[...] = rope(seg(5), 1.0)
    vt = seg(6).T.astype(BF16)
    for hh in range(N_HEADS):
        dvt_ref[hh, :HEAD_W, :] = vt[hh * HEAD_W:(hh + 1) * HEAD_W, :]
        dvt_ref[hh, HEAD_W:, :] = jnp.ones((VT_ROWS - HEAD_W, IN_TM), BF16)
    dgate_ref[...] = _silu(seg(7))


def _in_proj(x2, norm_w, w_in_bf, lb_logits, cos_t, sin_t):
    T, D = x2.shape
    n = T // IN_TM
    row = lambda i: (i, 0)
    fix = lambda i: (0, 0)
    f32_out = jax.ShapeDtypeStruct((T, GROUP_W), F32)
    bf_out = jax.ShapeDtypeStruct((T, GROUP_W), BF16)
    blk = pl.BlockSpec((IN_TM, GROUP_W), row)
    return pl.pallas_call(
        _in_proj_kernel,
        grid=(n,),
        in_specs=[
            pl.BlockSpec((IN_TM, D), row),
            pl.BlockSpec((1, D), fix),
            pl.BlockSpec(w_in_bf.shape, fix, pipeline_mode=pl.Buffered(1)),
            pl.BlockSpec(lb_logits.shape, fix),
            pl.BlockSpec((QK_DIM // 2, IN_TM), lambda i: (0, i)),
            pl.BlockSpec((QK_DIM // 2, IN_TM), lambda i: (0, i)),
        ],
        out_specs=[blk, blk, blk, blk, blk, blk, blk,
                   pl.BlockSpec((N_HEADS, VT_ROWS, IN_TM), lambda i: (0, 0, i)), blk],
        out_shape=[f32_out, f32_out, f32_out, bf_out, f32_out, bf_out, bf_out,
                   jax.ShapeDtypeStruct((N_HEADS, VT_ROWS, T), BF16), f32_out],
        compiler_params=pltpu.CompilerParams(
            dimension_semantics=("arbitrary",), vmem_limit_bytes=VMEM_LIMIT),
        name="in_proj",
    )(x2, norm_w, w_in_bf, lb_logits, cos_t, sin_t)


def _hgrn_kernel(q_ref, k_ref, g_ref, v_ref, gate_ref, nw_ref, o_ref, st_ref, c_ref):
    C, SB = HGRN_CHUNK, HGRN_SUB

    @pl.when(pl.program_id(0) == 0)
    def _():
        st_ref[...] = jnp.zeros_like(st_ref)

    ti = lax.broadcasted_iota(jnp.int32, (C, C), 0)
    si = lax.broadcasted_iota(jnp.int32, (C, C), 1)
    tri = (si <= ti).astype(F32)
    a_row = lax.broadcasted_iota(jnp.int32, (SUBLANES, C), 0)
    a_lane = lax.broadcasted_iota(jnp.int32, (SUBLANES, C), 1)
    nw = nw_ref[...]

    def chunk_heads(c, slot):
        r0 = pl.multiple_of(c * C, C)
        rows = pl.ds(r0, C)
        b_all = jnp.dot(tri, g_ref[rows, :], preferred_element_type=F32,
                        precision=lax.Precision.HIGHEST)
        c_ref[slot] = b_all - jnp.log2(jnp.maximum(k_ref[rows, :], 0.0))

        def head(hh):
            cols = slice(hh * HEAD_W, (hh + 1) * HEAD_W)
            b = b_all[:, cols]
            q = q_ref[rows, cols]
            k = k_ref[rows, cols]
            v_bf = v_ref[rows, cols]
            st = st_ref[hh]
            inter = lax.dot_general((q * jnp.exp2(b)).astype(BF16), st.astype(BF16),
                                    NT, preferred_element_type=F32)
            b_last = b[C - 1:C, :]
            k_dec = (k * jnp.exp2(b_last - b)).astype(BF16)
            st_ref[hh] = st * jnp.exp2(b_last) + lax.dot_general(
                v_bf, k_dec, TN, preferred_element_type=F32)

            q_near, k_near = [], []
            for blk in range(C // SB):
                lo, mid = blk * SB, blk * SB + SUBLANES
                ref = b[mid - 1:mid, :]
                q_near.append(q[mid:mid + SUBLANES, :] * jnp.exp2(b[mid:mid + SUBLANES, :] - ref))
                k_near.append(k[lo:mid, :] * jnp.exp2(ref - b[lo:mid, :]))
                k_near.append(jnp.zeros((SUBLANES, HEAD_W), F32))
            a_near = lax.dot_general(jnp.concatenate(q_near, axis=0).astype(BF16),
                                     jnp.concatenate(k_near, axis=0).astype(BF16),
                                     NT, preferred_element_type=F32)
            yield
            pieces = []
            for blk in range(C // SB):
                lo = blk * SB
                near = a_near[blk * SUBLANES:(blk + 1) * SUBLANES, :]
                if blk > 0:
                    ref = b[lo - 1:lo, :]
                    q_t = (q[lo:lo + SB, :] * jnp.exp2(b[lo:lo + SB, :] - ref)).astype(BF16)
                    k_t = (k[:lo, :] * jnp.exp2(ref - b[:lo, :])).astype(BF16)
                    k_t = jnp.concatenate([k_t, jnp.zeros((C - lo, HEAD_W), BF16)], axis=0)
                    a_far = lax.dot_general(q_t, k_t, NT, preferred_element_type=F32)
                    halves = [a_far[:SUBLANES, :],
                              jnp.where(a_lane >= lo, near, a_far[SUBLANES:, :])]
                else:
                    halves = [jnp.zeros((SUBLANES, C), F32), near]
                for s in range(SB):
                    half = s // SUBLANES
                    r = lo + half * SUBLANES
                    c_s = c_ref[slot, pl.ds(lo + s, 1), cols]
                    w = q[r:r + SUBLANES, :] * jnp.exp2(b[r:r + SUBLANES, :] - c_s)
                    col = jnp.sum(w, axis=-1, keepdims=True)
                    halves[half] = jnp.where(a_lane == lo + s, col, halves[half])
                for half in range(2):
                    keep = a_lane <= a_row + (lo + half * SUBLANES)
                    pieces.append(jnp.where(keep, halves[half], 0.0))
                yield
            a = jnp.concatenate(pieces, axis=0).astype(BF16)
            o = inter + jnp.dot(a, v_bf, preferred_element_type=F32)
            ms = jnp.mean(o * o, axis=-1, keepdims=True)
            o_ref[rows, cols] = (o * lax.rsqrt(ms + NORM_EPS) * nw
                                 * gate_ref[rows, cols]).astype(BF16)

        return [head(hh) for hh in range(N_HEADS)]

    def chunk_group(cg, carry):
        gens = []
        for j in range(HGRN_GROUP):
            gens += chunk_heads(HGRN_GROUP * cg + j, j)
        _run_interleaved(*gens)
        return carry

    n_groups = HGRN_TM // (HGRN_GROUP * C)
    lax.fori_loop(0, n_groups, chunk_group, 0, unroll=n_groups)


def _hgrn(hq, hk, hgl, hi, hgate, norm_w):
    T = hq.shape[0]
    blk = pl.BlockSpec((HGRN_TM, GROUP_W), lambda i: (i, 0))
    return pl.pallas_call(
        _hgrn_kernel,
        grid=(T // HGRN_TM,),
        in_specs=[blk, blk, blk, blk, blk, pl.BlockSpec((1, HEAD_W), lambda i: (0, 0))],
        out_specs=blk,
        out_shape=jax.ShapeDtypeStruct((T, GROUP_W), BF16),
        scratch_shapes=[pltpu.VMEM((N_HEADS, HEAD_W, HEAD_W), F32),
                        pltpu.VMEM((HGRN_GROUP, HGRN_CHUNK, GROUP_W), F32)],
        compiler_params=pltpu.CompilerParams(
            dimension_semantics=("arbitrary",), vmem_limit_bytes=VMEM_LIMIT),
        name="hgrn2",
    )(hq, hk, hgl, hi, hgate, norm_w)


def _attn_kernel(lq1_ref, lk1_ref, lq2_ref, lk2_ref, q_ref, k_ref, vt_ref, nw_ref,
                 o_ref, s_ref, acc_ref, *, lambda_init):
    n_q = q_ref.shape[0] // ATT_TQ
    n_steps = n_q * (n_q + 1) // 2
    assert n_steps % ATT_STEPS == 0 and ATT_STEPS % 2 == 0 and ATT_TQ == ATT_TK
    n_t = ATT_TQ // ATT_TN
    lam = (jnp.exp(jnp.sum(lq1_ref[...] * lk1_ref[...]))
           - jnp.exp(jnp.sum(lq2_ref[...] * lk2_ref[...])) + lambda_init)
    lane = lax.broadcasted_iota(jnp.int32, (ATT_TQ, HEAD_W), 1)
    neg_inf = jnp.full((1, ATT_TN), -jnp.inf, F32)

    acc_ref[...] = jnp.zeros_like(acc_ref)

    def advance(pos):
        qi, t = pos
        row_end = t == qi
        qi_n = jnp.where(row_end, qi + 1, qi)
        t_n = jnp.where(row_end, 0, t + 1)
        done = qi_n == n_q
        return jnp.where(done, n_q - 1, qi_n), jnp.where(done, n_q - 1, t_n)

    def diag_rows(n):
        return min(ATT_TK, (n + 1) * ATT_TN)

    def scores(pos, slot, masked):
        qi, t = pos
        q = q_ref[pl.ds(pl.multiple_of(qi * ATT_TQ, ATT_TQ), ATT_TQ), :]
        kb = k_ref[pl.ds(pl.multiple_of(t * ATT_TK, ATT_TK), ATT_TK), :]
        zero = jnp.zeros_like(q)
        q_maps = (jnp.where(lane < QK_DIM, q, zero), jnp.where(lane >= QK_DIM, q, zero))
        mx = []
        for c in range(2):
            for n in range(n_t):
                cols = slice(n * ATT_TN, (n + 1) * ATT_TN)
                live = diag_rows(n) if masked is True else ATT_TK
                s = lax.dot_general(kb[:live], q_maps[c][cols, :], NT,
                                    preferred_element_type=F32)
                if masked is not False:
                    kpos = lax.broadcasted_iota(jnp.int32, s.shape, 0) + t * ATT_TK
                    qpos = lax.broadcasted_iota(jnp.int32, s.shape, 1) + (qi * ATT_TQ + n * ATT_TN)
                    keep = kpos <= qpos
                    if masked is not True:
                        keep = keep | jnp.logical_not(masked)
                    s = jnp.where(keep, s, -jnp.inf)
                s_ref[slot, c, :live, cols] = s
                if live < ATT_TK:
                    s_ref[slot, c, live:, cols] = jnp.full((ATT_TK - live, ATT_TN), -jnp.inf, F32)
                mx.append(jnp.max(s, axis=0, keepdims=True))
                yield
        return tuple(mx)

    def softmax_pv(pos, slot, mx, m_run, diag=False):
        qi, t = pos
        first = t == 0
        a = qi % ACC_SLOTS
        vt = vt_ref[:, pl.ds(pl.multiple_of(t * ATT_TK, ATT_TK), ATT_TK)]
        m_out = []
        for c in range(2):
            for n in range(n_t):
                i = c * n_t + n
                cols = slice(n * ATT_TN, (n + 1) * ATT_TN)
                m_old = jnp.where(first, neg_inf, m_run[i])
                m_new = jnp.maximum(m_old, mx[i])
                alpha = jnp.exp2(m_old - m_new)
                live = diag_rows(n) if diag else ATT_TK
                p = jnp.exp2(s_ref[slot, c, :live, cols] - m_new).astype(BF16)
                upd = jnp.dot(vt[:, :live], p, preferred_element_type=F32)
                acc_ref[a, c, :, cols] = alpha * acc_ref[a, c, :, cols] + upd
                m_out.append(m_new)
                yield
        return tuple(m_out)

    def finalize(qi):
        a1 = acc_ref[qi % ACC_SLOTS, 0]
        a2 = acc_ref[qi % ACC_SLOTS, 1]
        o_t = (a1[:HEAD_W] / a1[HEAD_W:HEAD_W + 1]
               - lam * (a2[:HEAD_W] / a2[HEAD_W:HEAD_W + 1]))
        o = o_t.T
        ms = jnp.mean(o * o, axis=-1, keepdims=True)
        rows = pl.ds(pl.multiple_of(qi * ATT_TQ, ATT_TQ), ATT_TQ)
        o_ref[rows, :] = o * lax.rsqrt(ms + SUBLN_EPS) * nw_ref[...] * (1.0 - lambda_init)

    def on_diag(pos):
        return pos[0] == pos[1]

    def body_block(pos_a, pos_b, mx, m_run, masks, fin_step):
        for j in range(ATT_STEPS):
            mx, m_run = _run_interleaved(scores(pos_a[j], (j + 1) % 2, masks[j]),
                                         softmax_pv(pos_b[j], j % 2, mx, m_run, j == fin_step))
            if j == fin_step:
                finalize(pos_b[j][0])
        return mx, m_run

    def positions(pos):
        out = [pos]
        for _ in range(ATT_STEPS - 1):
            out.append(advance(out[-1]))
        return out

    def body(it, carry):
        pos_a0, pos_b0, mx, m_run = carry
        pos_a, pos_b = positions(pos_a0), positions(pos_b0)
        flags = [on_diag(pos_a[j]) & (it * ATT_STEPS + 1 + j < n_steps) for j in range(ATT_STEPS)]
        count = sum(f.astype(jnp.int32) for f in flags)
        single = sum((j + 1) * f.astype(jnp.int32) for j, f in enumerate(flags))
        idx = jnp.where(count > 1, ATT_STEPS + 1, single)
        mask_sets = ([[False] * ATT_STEPS]
                     + [[j == k for j in range(ATT_STEPS)] for k in range(ATT_STEPS)]
                     + [flags])
        fin_steps = ([None] + [k + 1 if k + 1 < ATT_STEPS else None for k in range(ATT_STEPS)]
                     + [None])
        branches = [functools.partial(
            lambda ms, fs, _: body_block(pos_a, pos_b, mx, m_run, ms, fs), ms, fs)
            for ms, fs in zip(mask_sets, fin_steps)]
        mx, m_run = lax.switch(idx, branches, 0)

        in_line = jnp.where((count == 1) & (single < ATT_STEPS), single, -1)
        for j in range(ATT_STEPS):
            @pl.when(on_diag(pos_b[j]) & (in_line != j))
            def _():
                finalize(pos_b[j][0])

        return advance(pos_a[-1]), advance(pos_b[-1]), mx, m_run

    pos0 = (jnp.int32(0), jnp.int32(0))
    mx0 = _run_interleaved(scores(pos0, 0, True))[0]
    m0 = (neg_inf,) * (2 * n_t)
    lax.fori_loop(0, n_steps // ATT_STEPS, body, (advance(pos0), pos0, mx0, m0))


def _diff_attn(lq1, lk1, lq2, lk2, dq, dk, dvt, norm_w, lambda_init):
    T = dq.shape[0]
    lam_spec = pl.BlockSpec((1, QK_DIM), lambda h: (0, 0))
    return pl.pallas_call(
        functools.partial(_attn_kernel, lambda_init=lambda_init),
        grid=(N_HEADS,),
        in_specs=[
            lam_spec, lam_spec, lam_spec, lam_spec,
            pl.BlockSpec((T, HEAD_W), lambda h: (0, h)),
            pl.BlockSpec((T, HEAD_W), lambda h: (0, h)),
            pl.BlockSpec((None, VT_ROWS, T), lambda h: (h, 0, 0)),
            pl.BlockSpec((1, HEAD_W), lambda h: (0, 0)),
        ],
        out_specs=pl.BlockSpec((T, HEAD_W), lambda h: (0, h)),
        out_shape=jax.ShapeDtypeStruct((T, GROUP_W), F32),
        scratch_shapes=[
            pltpu.VMEM((2, 2, ATT_TK, ATT_TQ), F32),
            pltpu.VMEM((ACC_SLOTS, 2, VT_ROWS, ATT_TQ), F32),
        ],
        compiler_params=pltpu.CompilerParams(
            dimension_semantics=("arbitrary",), vmem_limit_bytes=VMEM_LIMIT),
        name="diff_attn",
    )(lq1, lk1, lq2, lk2, dq, dk, dvt, norm_w)


def _out_proj_kernel(ho_ref, do_ref, dgate_ref, x_ref, w_ref, nw_ref, o_ref):
    y = x_ref[...]
    y = y + jnp.dot(ho_ref[...], w_ref[:GROUP_W, :], preferred_element_type=F32)
    mix_d = (do_ref[...] * dgate_ref[...]).astype(BF16)
    y = y + jnp.dot(mix_d, w_ref[GROUP_W:, :], preferred_element_type=F32)
    ms = jnp.mean(y * y, axis=-1, keepdims=True)
    o_ref[...] = y * lax.rsqrt(ms + NORM_EPS) * nw_ref[...]


def _out_proj(ho, do, dgate, x2, w_out_bf, final_norm_w):
    T, D = x2.shape
    row = lambda i: (i, 0)
    fix = lambda i: (0, 0)
    return pl.pallas_call(
        _out_proj_kernel,
        grid=(T // PROJ_TM,),
        in_specs=[
            pl.BlockSpec((PROJ_TM, GROUP_W), row),
            pl.BlockSpec((PROJ_TM, GROUP_W), row),
            pl.BlockSpec((PROJ_TM, GROUP_W), row),
            pl.BlockSpec((PROJ_TM, D), row),
            pl.BlockSpec(w_out_bf.shape, fix),
            pl.BlockSpec((1, D), fix),
        ],
        out_specs=pl.BlockSpec((PROJ_TM, D), row),
        out_shape=jax.ShapeDtypeStruct((T, D), F32),
        compiler_params=pltpu.CompilerParams(
            dimension_semantics=("arbitrary",), vmem_limit_bytes=VMEM_LIMIT),
        name="out_proj",
    )(ho, do, dgate, x2, w_out_bf, final_norm_w)


def _rope_tables(T):
    half = QK_DIM // 2
    inv_freq = 1.0 / (ROPE_THETA ** (jnp.arange(half, dtype=F32) / half))
    ang = inv_freq[:, None] * jnp.arange(T, dtype=F32)[None, :]
    return jnp.cos(ang), jnp.sin(ang)


def kernel(x, norm_w, w_in, hgrn_lb_logits, hgrn_norm_w, diff_lambda_q1, diff_lambda_k1,
           diff_lambda_q2, diff_lambda_k2, diff_norm_w, w_out, final_norm_w):
    B, T, D = x.shape
    depth = norm_w.shape[0]
    assert B == 1 and depth == 1 and D == 2 * GROUP_W
    assert T % ATT_TQ == 0 and T % HGRN_TM == 0 and T % PROJ_TM == 0 and T % IN_TM == 0
    x2 = x.reshape(T, D)
    cos_t, sin_t = _rope_tables(T)
    l = 0
    lambda_init = 0.8 - 0.6 * math.exp(-0.3 * l)
    hq, hk, hgl, hi, hgate, dq, dk, dvt, dgate = _in_proj(
        x2, norm_w[l:l + 1], w_in[l].astype(BF16), hgrn_lb_logits, cos_t, sin_t)
    ho = _hgrn(hq, hk, hgl, hi, hgate, hgrn_norm_w[l:l + 1])
    do = _diff_attn(diff_lambda_q1[l:l + 1], diff_lambda_k1[l:l + 1],
                    diff_lambda_q2[l:l + 1], diff_lambda_k2[l:l + 1],
                    dq, dk, dvt, diff_norm_w[l:l + 1], lambda_init)
    out = _out_proj(ho, do, dgate, x2, w_out[l].astype(BF16), final_norm_w.reshape(1, D))
    return out.reshape(B, T, D)
```

```python
import functools
import math

import jax
import jax.numpy as jnp
from jax import lax
from jax.experimental import pallas as pl
from jax.experimental.pallas import tpu as pltpu

F32 = jnp.float32
BF16 = jnp.bfloat16

SUBLANES = 8
VMEM_LIMIT = 56 * 1024 * 1024

N_HEADS = 4
HEAD_W = 128
GROUP_W = N_HEADS * HEAD_W
QK_DIM = 64
ROPE_THETA = 10000.0
NORM_EPS = 1e-6
SUBLN_EPS = 1e-5

IN_TM = 1024
PROJ_TM = 1024
HGRN_CHUNK = 64
HGRN_SUB = 16
HGRN_TM = 512
HGRN_GROUP = 8
ATT_TQ = 512
ATT_TK = 512
ATT_TN = 256
ATT_STEPS = 6
ACC_SLOTS = 4
BF16_ROWS = 16
VT_ROWS = HEAD_W + BF16_ROWS
LOG2E = math.log2(math.e)

NT = (((1,), (1,)), ((), ()))
TN = (((0,), (0,)), ((), ()))


def _run_interleaved(*gens):
    results = [None] * len(gens)
    live = list(range(len(gens)))
    while live:
        for i in list(live):
            try:
                next(gens[i])
            except StopIteration as stop:
                results[i] = stop.value
                live.remove(i)
    return results


def _silu(x):
    return x * (1.0 / (1.0 + jnp.exp(-x)))


def _in_proj_kernel(x_ref, nw_ref, w_ref, lbl_ref, cos_a_ref, sin_a_ref, cos_b_ref, sin_b_ref,
                    hq_ref, hk_ref, hgl_ref, hi_ref, hgate_ref,
                    dq_ref, dk_ref, dvt_ref, dgate_ref):
    x = x_ref[...]
    ms = jnp.mean(x * x, axis=-1, keepdims=True)
    h = (x * lax.rsqrt(ms + NORM_EPS) * nw_ref[...]).astype(BF16)

    def seg(i):
        return jnp.dot(h, w_ref[:, i * GROUP_W:(i + 1) * GROUP_W],
                       preferred_element_type=F32)

    lg = lbl_ref[...]
    e = jnp.exp(lg - jnp.max(lg, axis=0, keepdims=True))
    lb = e[0:1, :] / jnp.sum(e, axis=0, keepdims=True)

    hq_ref[...] = seg(0)
    f = lb + (1.0 - lb) * (1.0 / (1.0 + jnp.exp(-seg(1))))
    hk_ref[...] = 1.0 - f
    hgl_ref[...] = jnp.log2(f)
    hi_ref[...] = seg(2).astype(BF16)
    hgate_ref[...] = _silu(seg(3))

    reps = HEAD_W // (QK_DIM // 2)
    cos_a, sin_a = cos_a_ref[...], sin_a_ref[...]
    cos_b, sin_b = cos_b_ref[...], sin_b_ref[...]
    cos_half = cos_a * cos_b - sin_a * sin_b
    sin_half = sin_a * cos_b + cos_a * sin_b
    cos = jnp.concatenate([cos_half] * reps, axis=0).T
    sin = jnp.concatenate([-sin_half, sin_half] * (reps // 2), axis=0).T
    lane = lax.broadcasted_iota(jnp.int32, cos.shape, 1)
    first_half = (lane % QK_DIM) < (QK_DIM // 2)

    def rope(t, scale):
        outs = []
        for hh in range(N_HEADS):
            th = t[:, hh * HEAD_W:(hh + 1) * HEAD_W]
            up = pltpu.roll(th, HEAD_W - QK_DIM // 2, 1)
            dn = pltpu.roll(th, QK_DIM // 2, 1)
            partner = jnp.where(first_half, up, dn)
            outs.append(((th * cos + partner * sin) * scale).astype(BF16))
        return jnp.concatenate(outs, axis=1)

    dq_ref[...] = rope(seg(4), QK_DIM ** -0.5 * LOG2E)
    dk_ref[...] = rope(seg(5), 1.0)
    vt = seg(6).T.astype(BF16)
    for hh in range(N_HEADS):
        dvt_ref[hh, :HEAD_W, :] = vt[hh * HEAD_W:(hh + 1) * HEAD_W, :]
        dvt_ref[hh, HEAD_W:, :] = jnp.ones((VT_ROWS - HEAD_W, IN_TM), BF16)
    dgate_ref[...] = _silu(seg(7))


def _in_proj(x2, norm_w, w_in_bf, lb_logits, rope):
    T, D = x2.shape
    n = T // IN_TM
    row = lambda i: (i, 0)
    fix = lambda i: (0, 0)
    f32_out = jax.ShapeDtypeStruct((T, GROUP_W), F32)
    bf_out = jax.ShapeDtypeStruct((T, GROUP_W), BF16)
    blk = pl.BlockSpec((IN_TM, GROUP_W), row)
    return pl.pallas_call(
        _in_proj_kernel,
        grid=(n,),
        in_specs=[
            pl.BlockSpec((IN_TM, D), row),
            pl.BlockSpec((1, D), fix),
            pl.BlockSpec(w_in_bf.shape, fix, pipeline_mode=pl.Buffered(1)),
            pl.BlockSpec(lb_logits.shape, fix),
            pl.BlockSpec((None, QK_DIM // 2, 1), lambda i: (i, 0, 0)),
            pl.BlockSpec((None, QK_DIM // 2, 1), lambda i: (i, 0, 0)),
            pl.BlockSpec((QK_DIM // 2, IN_TM), fix),
            pl.BlockSpec((QK_DIM // 2, IN_TM), fix),
        ],
        out_specs=[blk, blk, blk, blk, blk, blk, blk,
                   pl.BlockSpec((N_HEADS, VT_ROWS, IN_TM), lambda i: (0, 0, i)), blk],
        out_shape=[f32_out, f32_out, f32_out, bf_out, f32_out, bf_out, bf_out,
                   jax.ShapeDtypeStruct((N_HEADS, VT_ROWS, T), BF16), f32_out],
        compiler_params=pltpu.CompilerParams(
            dimension_semantics=("arbitrary",), vmem_limit_bytes=VMEM_LIMIT),
        name="in_proj",
    )(x2, norm_w, w_in_bf, lb_logits, *rope)


def _hgrn_kernel(q_ref, k_ref, g_ref, v_ref, gate_ref, nw_ref, o_ref, st_ref, c_ref):
    C, SB = HGRN_CHUNK, HGRN_SUB

    @pl.when(pl.program_id(0) == 0)
    def _():
        st_ref[...] = jnp.zeros_like(st_ref)

    ti = lax.broadcasted_iota(jnp.int32, (C, C), 0)
    si = lax.broadcasted_iota(jnp.int32, (C, C), 1)
    tri = (si <= ti).astype(F32)
    a_row = lax.broadcasted_iota(jnp.int32, (SUBLANES, C), 0)
    a_lane = lax.broadcasted_iota(jnp.int32, (SUBLANES, C), 1)
    nw = nw_ref[...]

    def chunk_heads(c, slot):
        r0 = pl.multiple_of(c * C, C)
        rows = pl.ds(r0, C)
        b_all = jnp.dot(tri, g_ref[rows, :], preferred_element_type=F32,
                        precision=lax.Precision.HIGHEST)
        c_ref[slot] = b_all - jnp.log2(jnp.maximum(k_ref[rows, :], 0.0))

        def head(hh):
            cols = slice(hh * HEAD_W, (hh + 1) * HEAD_W)
            b = b_all[:, cols]
            q = q_ref[rows, cols]
            k = k_ref[rows, cols]
            v_bf = v_ref[rows, cols]
            st = st_ref[hh]
            inter = lax.dot_general((q * jnp.exp2(b)).astype(BF16), st.astype(BF16),
                                    NT, preferred_element_type=F32)
            b_last = b[C - 1:C, :]
            k_dec = (k * jnp.exp2(b_last - b)).astype(BF16)
            st_ref[hh] = st * jnp.exp2(b_last) + lax.dot_general(
                v_bf, k_dec, TN, preferred_element_type=F32)

            q_near, k_near = [], []
            for blk in range(C // SB):
                lo, mid = blk * SB, blk * SB + SUBLANES
                ref = b[mid - 1:mid, :]
                q_near.append(q[mid:mid + SUBLANES, :] * jnp.exp2(b[mid:mid + SUBLANES, :] - ref))
                k_near.append(k[lo:mid, :] * jnp.exp2(ref - b[lo:mid, :]))
                k_near.append(jnp.zeros((SUBLANES, HEAD_W), F32))
            a_near = lax.dot_general(jnp.concatenate(q_near, axis=0).astype(BF16),
                                     jnp.concatenate(k_near, axis=0).astype(BF16),
                                     NT, preferred_element_type=F32)
            yield
            pieces = []
            for blk in range(C // SB):
                lo = blk * SB
                near = a_near[blk * SUBLANES:(blk + 1) * SUBLANES, :]
                if blk > 0:
                    ref = b[lo - 1:lo, :]
                    q_t = (q[lo:lo + SB, :] * jnp.exp2(b[lo:lo + SB, :] - ref)).astype(BF16)
                    k_t = (k[:lo, :] * jnp.exp2(ref - b[:lo, :])).astype(BF16)
                    k_t = jnp.concatenate([k_t, jnp.zeros((C - lo, HEAD_W), BF16)], axis=0)
                    a_far = lax.dot_general(q_t, k_t, NT, preferred_element_type=F32)
                    halves = [a_far[:SUBLANES, :],
                              jnp.where(a_lane >= lo, near, a_far[SUBLANES:, :])]
                else:
                    halves = [jnp.zeros((SUBLANES, C), F32), near]
                for s in range(SB):
                    half = s // SUBLANES
                    r = lo + half * SUBLANES
                    c_s = c_ref[slot, pl.ds(lo + s, 1), cols]
                    w = q[r:r + SUBLANES, :] * jnp.exp2(b[r:r + SUBLANES, :] - c_s)
                    col = jnp.sum(w, axis=-1, keepdims=True)
                    halves[half] = jnp.where(a_lane == lo + s, col, halves[half])
                for half in range(2):
                    keep = a_lane <= a_row + (lo + half * SUBLANES)
                    pieces.append(jnp.where(keep, halves[half], 0.0))
                yield
            a = jnp.concatenate(pieces, axis=0).astype(BF16)
            o = inter + jnp.dot(a, v_bf, preferred_element_type=F32)
            ms = jnp.mean(o * o, axis=-1, keepdims=True)
            o_ref[rows, cols] = (o * lax.rsqrt(ms + NORM_EPS) * nw
                                 * gate_ref[rows, cols]).astype(BF16)

        return [head(hh) for hh in range(N_HEADS)]

    def chunk_group(cg, carry):
        gens = []
        for j in range(HGRN_GROUP):
            gens += chunk_heads(HGRN_GROUP * cg + j, j)
        _run_interleaved(*gens)
        return carry

    n_groups = HGRN_TM // (HGRN_GROUP * C)
    lax.fori_loop(0, n_groups, chunk_group, 0, unroll=n_groups)


def _hgrn(hq, hk, hgl, hi, hgate, norm_w):
    T = hq.shape[0]
    blk = pl.BlockSpec((HGRN_TM, GROUP_W), lambda i: (i, 0))
    return pl.pallas_call(
        _hgrn_kernel,
        grid=(T // HGRN_TM,),
        in_specs=[blk, blk, blk, blk, blk, pl.BlockSpec((1, HEAD_W), lambda i: (0, 0))],
        out_specs=blk,
        out_shape=jax.ShapeDtypeStruct((T, GROUP_W), BF16),
        scratch_shapes=[pltpu.VMEM((N_HEADS, HEAD_W, HEAD_W), F32),
                        pltpu.VMEM((HGRN_GROUP, HGRN_CHUNK, GROUP_W), F32)],
        compiler_params=pltpu.CompilerParams(
            dimension_semantics=("arbitrary",), vmem_limit_bytes=VMEM_LIMIT),
        name="hgrn2",
    )(hq, hk, hgl, hi, hgate, norm_w)


def _attn_kernel(lq1_ref, lk1_ref, lq2_ref, lk2_ref, q_ref, k_ref, vt_ref, nw_ref,
                 o_ref, s_ref, acc_ref, *, lambda_init):
    n_q = q_ref.shape[0] // ATT_TQ
    n_steps = n_q * (n_q + 1) // 2
    assert n_steps % ATT_STEPS == 0 and ATT_STEPS % 2 == 0 and ATT_TQ == ATT_TK
    n_t = ATT_TQ // ATT_TN
    lam = (jnp.exp(jnp.sum(lq1_ref[...] * lk1_ref[...]))
           - jnp.exp(jnp.sum(lq2_ref[...] * lk2_ref[...])) + lambda_init)
    lane = lax.broadcasted_iota(jnp.int32, (ATT_TQ, HEAD_W), 1)
    neg_inf = jnp.full((1, ATT_TN), -jnp.inf, F32)

    acc_ref[...] = jnp.zeros_like(acc_ref)

    def advance(pos):
        qi, t = pos
        row_end = t == qi
        qi_n = jnp.where(row_end, qi + 1, qi)
        t_n = jnp.where(row_end, 0, t + 1)
        done = qi_n == n_q
        return jnp.where(done, n_q - 1, qi_n), jnp.where(done, n_q - 1, t_n)

    def diag_rows(n):
        return min(ATT_TK, (n + 1) * ATT_TN)

    def scores(pos, slot, masked):
        qi, t = pos
        q = q_ref[pl.ds(pl.multiple_of(qi * ATT_TQ, ATT_TQ), ATT_TQ), :]
        kb = k_ref[pl.ds(pl.multiple_of(t * ATT_TK, ATT_TK), ATT_TK), :]
        zero = jnp.zeros_like(q)
        q_maps = (jnp.where(lane < QK_DIM, q, zero), jnp.where(lane >= QK_DIM, q, zero))
        mx = []
        for c in range(2):
            for n in range(n_t):
                cols = slice(n * ATT_TN, (n + 1) * ATT_TN)
                live = diag_rows(n) if masked is True else ATT_TK
                s = lax.dot_general(kb[:live], q_maps[c][cols, :], NT,
                                    preferred_element_type=F32)
                if masked is not False:
                    kpos = lax.broadcasted_iota(jnp.int32, s.shape, 0) + t * ATT_TK
                    qpos = lax.broadcasted_iota(jnp.int32, s.shape, 1) + (qi * ATT_TQ + n * ATT_TN)
                    keep = kpos <= qpos
                    if masked is not True:
                        keep = keep | jnp.logical_not(masked)
                    s = jnp.where(keep, s, -jnp.inf)
                s_ref[slot, c, :live, cols] = s
                if live < ATT_TK:
                    s_ref[slot, c, live:, cols] = jnp.full((ATT_TK - live, ATT_TN), -jnp.inf, F32)
                mx.append(jnp.max(s, axis=0, keepdims=True))
                yield
        return tuple(mx)

    def softmax_pv(pos, slot, mx, m_run, diag=False):
        qi, t = pos
        first = t == 0
        a = qi % ACC_SLOTS
        vt = vt_ref[:, pl.ds(pl.multiple_of(t * ATT_TK, ATT_TK), ATT_TK)]
        m_out = []
        for c in range(2):
            for n in range(n_t):
                i = c * n_t + n
                cols = slice(n * ATT_TN, (n + 1) * ATT_TN)
                m_old = jnp.where(first, neg_inf, m_run[i])
                m_new = jnp.maximum(m_old, mx[i])
                alpha = jnp.exp2(m_old - m_new)
                live = diag_rows(n) if diag else ATT_TK
                p = jnp.exp2(s_ref[slot, c, :live, cols] - m_new).astype(BF16)
                upd = jnp.dot(vt[:, :live], p, preferred_element_type=F32)
                acc_ref[a, c, :, cols] = alpha * acc_ref[a, c, :, cols] + upd
                m_out.append(m_new)
                yield
        return tuple(m_out)

    def finalize(qi):
        a1 = acc_ref[qi % ACC_SLOTS, 0]
        a2 = acc_ref[qi % ACC_SLOTS, 1]
        o_t = (a1[:HEAD_W] / a1[HEAD_W:HEAD_W + 1]
               - lam * (a2[:HEAD_W] / a2[HEAD_W:HEAD_W + 1]))
        o = o_t.T
        ms = jnp.mean(o * o, axis=-1, keepdims=True)
        rows = pl.ds(pl.multiple_of(qi * ATT_TQ, ATT_TQ), ATT_TQ)
        o_ref[rows, :] = o * lax.rsqrt(ms + SUBLN_EPS) * nw_ref[...] * (1.0 - lambda_init)

    def on_diag(pos):
        return pos[0] == pos[1]

    def body_block(pos_a, pos_b, mx, m_run, masks, fin_step):
        for j in range(ATT_STEPS):
            mx, m_run = _run_interleaved(scores(pos_a[j], (j + 1) % 2, masks[j]),
                                         softmax_pv(pos_b[j], j % 2, mx, m_run, j == fin_step))
            if j == fin_step:
                finalize(pos_b[j][0])
        return mx, m_run

    def positions(pos):
        out = [pos]
        for _ in range(ATT_STEPS - 1):
            out.append(advance(out[-1]))
        return out

    def body(it, carry):
        pos_a0, pos_b0, mx, m_run = carry
        pos_a, pos_b = positions(pos_a0), positions(pos_b0)
        flags = [on_diag(pos_a[j]) & (it * ATT_STEPS + 1 + j < n_steps) for j in range(ATT_STEPS)]
        count = sum(f.astype(jnp.int32) for f in flags)
        single = sum((j + 1) * f.astype(jnp.int32) for j, f in enumerate(flags))
        idx = jnp.where(count > 1, ATT_STEPS + 1, single)
        mask_sets = ([[False] * ATT_STEPS]
                     + [[j == k for j in range(ATT_STEPS)] for k in range(ATT_STEPS)]
                     + [flags])
        fin_steps = ([None] + [k + 1 if k + 1 < ATT_STEPS else None for k in range(ATT_STEPS)]
                     + [None])
        branches = [functools.partial(
            lambda ms, fs, _: body_block(pos_a, pos_b, mx, m_run, ms, fs), ms, fs)
            for ms, fs in zip(mask_sets, fin_steps)]
        mx, m_run = lax.switch(idx, branches, 0)

        in_line = jnp.where((count == 1) & (single < ATT_STEPS), single, -1)
        for j in range(ATT_STEPS):
            @pl.when(on_diag(pos_b[j]) & (in_line != j))
            def _():
                finalize(pos_b[j][0])

        return advance(pos_a[-1]), advance(pos_b[-1]), mx, m_run

    pos0 = (jnp.int32(0), jnp.int32(0))
    mx0 = _run_interleaved(scores(pos0, 0, True))[0]
    m0 = (neg_inf,) * (2 * n_t)
    lax.fori_loop(0, n_steps // ATT_STEPS, body, (advance(pos0), pos0, mx0, m0))


def _diff_attn(lq1, lk1, lq2, lk2, dq, dk, dvt, norm_w, lambda_init):
    T = dq.shape[0]
    lam_spec = pl.BlockSpec((1, QK_DIM), lambda h: (0, 0))
    return pl.pallas_call(
        functools.partial(_attn_kernel, lambda_init=lambda_init),
        grid=(N_HEADS,),
        in_specs=[
            lam_spec, lam_spec, lam_spec, lam_spec,
            pl.BlockSpec((T, HEAD_W), lambda h: (0, h)),
            pl.BlockSpec((T, HEAD_W), lambda h: (0, h)),
            pl.BlockSpec((None, VT_ROWS, T), lambda h: (h, 0, 0)),
            pl.BlockSpec((1, HEAD_W), lambda h: (0, 0)),
        ],
        out_specs=pl.BlockSpec((T, HEAD_W), lambda h: (0, h)),
        out_shape=jax.ShapeDtypeStruct((T, GROUP_W), F32),
        scratch_shapes=[
            pltpu.VMEM((2, 2, ATT_TK, ATT_TQ), F32),
            pltpu.VMEM((ACC_SLOTS, 2, VT_ROWS, ATT_TQ), F32),
        ],
        compiler_params=pltpu.CompilerParams(
            dimension_semantics=("arbitrary",), vmem_limit_bytes=VMEM_LIMIT),
        name="diff_attn",
    )(lq1, lk1, lq2, lk2, dq, dk, dvt, norm_w)


def _out_proj_kernel(ho_ref, do_ref, dgate_ref, x_ref, w_ref, nw_ref, o_ref):
    y = x_ref[...]
    y = y + jnp.dot(ho_ref[...], w_ref[:GROUP_W, :], preferred_element_type=F32)
    mix_d = (do_ref[...] * dgate_ref[...]).astype(BF16)
    y = y + jnp.dot(mix_d, w_ref[GROUP_W:, :], preferred_element_type=F32)
    ms = jnp.mean(y * y, axis=-1, keepdims=True)
    o_ref[...] = y * lax.rsqrt(ms + NORM_EPS) * nw_ref[...]


def _out_proj(ho, do, dgate, x2, w_out_bf, final_norm_w):
    T, D = x2.shape
    row = lambda i: (i, 0)
    fix = lambda i: (0, 0)
    return pl.pallas_call(
        _out_proj_kernel,
        grid=(T // PROJ_TM,),
        in_specs=[
            pl.BlockSpec((PROJ_TM, GROUP_W), row),
            pl.BlockSpec((PROJ_TM, GROUP_W), row),
            pl.BlockSpec((PROJ_TM, GROUP_W), row),
            pl.BlockSpec((PROJ_TM, D), row),
            pl.BlockSpec(w_out_bf.shape, fix),
            pl.BlockSpec((1, D), fix),
        ],
        out_specs=pl.BlockSpec((PROJ_TM, D), row),
        out_shape=jax.ShapeDtypeStruct((T, D), F32),
        compiler_params=pltpu.CompilerParams(
            dimension_semantics=("arbitrary",), vmem_limit_bytes=VMEM_LIMIT),
        name="out_proj",
    )(ho, do, dgate, x2, w_out_bf, final_norm_w)


def _rope_tables(T):
    half = QK_DIM // 2
    inv_freq = 1.0 / (ROPE_THETA ** (jnp.arange(half, dtype=F32) / half))
    start = inv_freq[None, :, None] * (IN_TM * jnp.arange(T // IN_TM, dtype=F32))[:, None, None]
    offset = inv_freq[:, None] * jnp.arange(IN_TM, dtype=F32)[None, :]
    return jnp.cos(start), jnp.sin(start), jnp.cos(offset), jnp.sin(offset)


def kernel(x, norm_w, w_in, hgrn_lb_logits, hgrn_norm_w, diff_lambda_q1, diff_lambda_k1,
           diff_lambda_q2, diff_lambda_k2, diff_norm_w, w_out, final_norm_w):
    B, T, D = x.shape
    depth = norm_w.shape[0]
    assert B == 1 and depth == 1 and D == 2 * GROUP_W
    assert T % ATT_TQ == 0 and T % HGRN_TM == 0 and T % PROJ_TM == 0 and T % IN_TM == 0
    x2 = x.reshape(T, D)
    rope = _rope_tables(T)
    l = 0
    lambda_init = 0.8 - 0.6 * math.exp(-0.3 * l)
    hq, hk, hgl, hi, hgate, dq, dk, dvt, dgate = _in_proj(
        x2, norm_w[l:l + 1], w_in[l].astype(BF16), hgrn_lb_logits, rope)
    ho = _hgrn(hq, hk, hgl, hi, hgate, hgrn_norm_w[l:l + 1])
    do = _diff_attn(diff_lambda_q1[l:l + 1], diff_lambda_k1[l:l + 1],
                    diff_lambda_q2[l:l + 1], diff_lambda_k2[l:l + 1],
                    dq, dk, dvt, diff_norm_w[l:l + 1], lambda_init)
    out = _out_proj(ho, do, dgate, x2, w_out[l].astype(BF16), final_norm_w.reshape(1, D))
    return out.reshape(B, T, D)
```

```python
import functools
import math

import jax
import jax.numpy as jnp
from jax import lax
from jax.experimental import pallas as pl
from jax.experimental.pallas import tpu as pltpu

F32 = jnp.float32
BF16 = jnp.bfloat16

SUBLANES = 8
VMEM_LIMIT = 56 * 1024 * 1024

N_HEADS = 4
HEAD_W = 128
GROUP_W = N_HEADS * HEAD_W
QK_DIM = 64
ROPE_THETA = 10000.0
NORM_EPS = 1e-6
SUBLN_EPS = 1e-5

IN_TM = 1024
PROJ_TM = 1024
HGRN_CHUNK = 64
HGRN_SUB = 16
HGRN_TM = 512
HGRN_GROUP = 8
ATT_TQ = 512
ATT_TK = 512
ATT_TN = 256
ATT_STEPS = 6
GATE_SLOTS = 8
ACC_SLOTS = 4
BF16_ROWS = 16
VT_ROWS = HEAD_W + BF16_ROWS
LOG2E = math.log2(math.e)

NT = (((1,), (1,)), ((), ()))
TN = (((0,), (0,)), ((), ()))


def _run_interleaved(*gens):
    results = [None] * len(gens)
    live = list(range(len(gens)))
    while live:
        for i in list(live):
            try:
                next(gens[i])
            except StopIteration as stop:
                results[i] = stop.value
                live.remove(i)
    return results


def _silu(x):
    return x * (1.0 / (1.0 + jnp.exp(-x)))


def _in_proj_kernel(x_ref, nw_ref, w_ref, lbl_ref, cos_a_ref, sin_a_ref, cos_b_ref, sin_b_ref,
                    hq_ref, hk_ref, hgl_ref, hi_ref, hgate_ref,
                    dq_ref, dk_ref, dvt_ref, dgate_ref):
    x = x_ref[...]
    ms = jnp.mean(x * x, axis=-1, keepdims=True)
    h = (x * lax.rsqrt(ms + NORM_EPS) * nw_ref[...]).astype(BF16)

    def seg(i):
        return jnp.dot(h, w_ref[:, i * GROUP_W:(i + 1) * GROUP_W],
                       preferred_element_type=F32)

    lg = lbl_ref[...]
    e = jnp.exp(lg - jnp.max(lg, axis=0, keepdims=True))
    lb = e[0:1, :] / jnp.sum(e, axis=0, keepdims=True)

    hq_ref[...] = seg(0)
    f = lb + (1.0 - lb) * (1.0 / (1.0 + jnp.exp(-seg(1))))
    hk_ref[...] = 1.0 - f
    hgl_ref[...] = jnp.log2(f)
    hi_ref[...] = seg(2).astype(BF16)
    hgate_ref[...] = _silu(seg(3))

    reps = HEAD_W // (QK_DIM // 2)
    cos_a, sin_a = cos_a_ref[...], sin_a_ref[...]
    cos_b, sin_b = cos_b_ref[...], sin_b_ref[...]
    cos_half = cos_a * cos_b - sin_a * sin_b
    sin_half = sin_a * cos_b + cos_a * sin_b
    cos = jnp.concatenate([cos_half] * reps, axis=0).T
    sin = jnp.concatenate([-sin_half, sin_half] * (reps // 2), axis=0).T
    lane = lax.broadcasted_iota(jnp.int32, cos.shape, 1)
    first_half = (lane % QK_DIM) < (QK_DIM // 2)

    def rope(t, scale):
        outs = []
        for hh in range(N_HEADS):
            th = t[:, hh * HEAD_W:(hh + 1) * HEAD_W]
            up = pltpu.roll(th, HEAD_W - QK_DIM // 2, 1)
            dn = pltpu.roll(th, QK_DIM // 2, 1)
            partner = jnp.where(first_half, up, dn)
            outs.append(((th * cos + partner * sin) * scale).astype(BF16))
        return jnp.concatenate(outs, axis=1)

    dq_ref[...] = rope(seg(4), QK_DIM ** -0.5 * LOG2E)
    dk_ref[...] = rope(seg(5), 1.0)
    vt = seg(6).T.astype(BF16)
    for hh in range(N_HEADS):
        dvt_ref[hh, :HEAD_W, :] = vt[hh * HEAD_W:(hh + 1) * HEAD_W, :]
        dvt_ref[hh, HEAD_W:, :] = jnp.ones((VT_ROWS - HEAD_W, IN_TM), BF16)
    dgate_ref[...] = _silu(seg(7))


def _in_proj(x2, norm_w, w_in_bf, lb_logits, rope):
    T, D = x2.shape
    n = T // IN_TM
    row = lambda i: (i, 0)
    fix = lambda i: (0, 0)
    f32_out = jax.ShapeDtypeStruct((T, GROUP_W), F32)
    bf_out = jax.ShapeDtypeStruct((T, GROUP_W), BF16)
    blk = pl.BlockSpec((IN_TM, GROUP_W), row)
    return pl.pallas_call(
        _in_proj_kernel,
        grid=(n,),
        in_specs=[
            pl.BlockSpec((IN_TM, D), row),
            pl.BlockSpec((1, D), fix),
            pl.BlockSpec(w_in_bf.shape, fix, pipeline_mode=pl.Buffered(1)),
            pl.BlockSpec(lb_logits.shape, fix),
            pl.BlockSpec((None, QK_DIM // 2, 1), lambda i: (i, 0, 0)),
            pl.BlockSpec((None, QK_DIM // 2, 1), lambda i: (i, 0, 0)),
            pl.BlockSpec((QK_DIM // 2, IN_TM), fix),
            pl.BlockSpec((QK_DIM // 2, IN_TM), fix),
        ],
        out_specs=[blk, blk, blk, blk, blk, blk, blk,
                   pl.BlockSpec((N_HEADS, VT_ROWS, IN_TM), lambda i: (0, 0, i)), blk],
        out_shape=[f32_out, f32_out, f32_out, bf_out, f32_out, bf_out, bf_out,
                   jax.ShapeDtypeStruct((N_HEADS, VT_ROWS, T), BF16), f32_out],
        compiler_params=pltpu.CompilerParams(
            dimension_semantics=("arbitrary",), vmem_limit_bytes=VMEM_LIMIT),
        name="in_proj",
    )(x2, norm_w, w_in_bf, lb_logits, *rope)


def _hgrn_kernel(q_ref, k_ref, g_ref, v_ref, gate_ref, nw_ref, o_ref, st_ref, c_ref):
    C, SB = HGRN_CHUNK, HGRN_SUB

    @pl.when(pl.program_id(0) == 0)
    def _():
        st_ref[...] = jnp.zeros_like(st_ref)

    ti = lax.broadcasted_iota(jnp.int32, (C, C), 0)
    si = lax.broadcasted_iota(jnp.int32, (C, C), 1)
    tri = (si <= ti).astype(F32)
    a_row = lax.broadcasted_iota(jnp.int32, (SUBLANES, C), 0)
    a_lane = lax.broadcasted_iota(jnp.int32, (SUBLANES, C), 1)
    nw = nw_ref[...]

    def chunk_heads(c, slot):
        r0 = pl.multiple_of(c * C, C)
        rows = pl.ds(r0, C)
        b_all = jnp.dot(tri, g_ref[rows, :], preferred_element_type=F32,
                        precision=lax.Precision.HIGHEST)
        c_ref[slot] = b_all - jnp.log2(jnp.maximum(k_ref[rows, :], 0.0))

        def head(hh):
            cols = slice(hh * HEAD_W, (hh + 1) * HEAD_W)
            b = b_all[:, cols]
            q = q_ref[rows, cols]
            k = k_ref[rows, cols]
            v_bf = v_ref[rows, cols]
            st = st_ref[hh]
            inter = lax.dot_general((q * jnp.exp2(b)).astype(BF16), st.astype(BF16),
                                    NT, preferred_element_type=F32)
            b_last = b[C - 1:C, :]
            k_dec = (k * jnp.exp2(b_last - b)).astype(BF16)
            st_ref[hh] = st * jnp.exp2(b_last) + lax.dot_general(
                v_bf, k_dec, TN, preferred_element_type=F32)

            q_near, k_near = [], []
            for blk in range(C // SB):
                lo, mid = blk * SB, blk * SB + SUBLANES
                ref = b[mid - 1:mid, :]
                q_near.append(q[mid:mid + SUBLANES, :] * jnp.exp2(b[mid:mid + SUBLANES, :] - ref))
                k_near.append(k[lo:mid, :] * jnp.exp2(ref - b[lo:mid, :]))
                k_near.append(jnp.zeros((SUBLANES, HEAD_W), F32))
            a_near = lax.dot_general(jnp.concatenate(q_near, axis=0).astype(BF16),
                                     jnp.concatenate(k_near, axis=0).astype(BF16),
                                     NT, preferred_element_type=F32)
            yield
            pieces = []
            for blk in range(C // SB):
                lo = blk * SB
                near = a_near[blk * SUBLANES:(blk + 1) * SUBLANES, :]
                if blk > 0:
                    ref = b[lo - 1:lo, :]
                    q_t = (q[lo:lo + SB, :] * jnp.exp2(b[lo:lo + SB, :] - ref)).astype(BF16)
                    k_t = (k[:lo, :] * jnp.exp2(ref - b[:lo, :])).astype(BF16)
                    k_t = jnp.concatenate([k_t, jnp.zeros((C - lo, HEAD_W), BF16)], axis=0)
                    a_far = lax.dot_general(q_t, k_t, NT, preferred_element_type=F32)
                    halves = [a_far[:SUBLANES, :],
                              jnp.where(a_lane >= lo, near, a_far[SUBLANES:, :])]
                else:
                    halves = [jnp.zeros((SUBLANES, C), F32), near]
                for s in range(SB):
                    half = s // SUBLANES
                    r = lo + half * SUBLANES
                    c_s = c_ref[slot, pl.ds(lo + s, 1), cols]
                    w = q[r:r + SUBLANES, :] * jnp.exp2(b[r:r + SUBLANES, :] - c_s)
                    col = jnp.sum(w, axis=-1, keepdims=True)
                    halves[half] = jnp.where(a_lane == lo + s, col, halves[half])
                for half in range(2):
                    keep = a_lane <= a_row + (lo + half * SUBLANES)
                    pieces.append(jnp.where(keep, halves[half], 0.0))
                yield
            a = jnp.concatenate(pieces, axis=0).astype(BF16)
            o = inter + jnp.dot(a, v_bf, preferred_element_type=F32)
            ms = jnp.mean(o * o, axis=-1, keepdims=True)
            o_ref[rows, cols] = (o * lax.rsqrt(ms + NORM_EPS) * nw
                                 * gate_ref[rows, cols]).astype(BF16)

        return [head(hh) for hh in range(N_HEADS)]

    def chunk_group(cg, carry):
        gens = []
        for j in range(HGRN_GROUP):
            gens += chunk_heads(HGRN_GROUP * cg + j, j)
        _run_interleaved(*gens)
        return carry

    n_groups = HGRN_TM // (HGRN_GROUP * C)
    lax.fori_loop(0, n_groups, chunk_group, 0, unroll=n_groups)


def _hgrn(hq, hk, hgl, hi, hgate, norm_w):
    T = hq.shape[0]
    blk = pl.BlockSpec((HGRN_TM, GROUP_W), lambda i: (i, 0))
    return pl.pallas_call(
        _hgrn_kernel,
        grid=(T // HGRN_TM,),
        in_specs=[blk, blk, blk, blk, blk, pl.BlockSpec((1, HEAD_W), lambda i: (0, 0))],
        out_specs=blk,
        out_shape=jax.ShapeDtypeStruct((T, GROUP_W), BF16),
        scratch_shapes=[pltpu.VMEM((N_HEADS, HEAD_W, HEAD_W), F32),
                        pltpu.VMEM((HGRN_GROUP, HGRN_CHUNK, GROUP_W), F32)],
        compiler_params=pltpu.CompilerParams(
            dimension_semantics=("arbitrary",), vmem_limit_bytes=VMEM_LIMIT),
        name="hgrn2",
    )(hq, hk, hgl, hi, hgate, norm_w)


def _attn_kernel(lq1_ref, lk1_ref, lq2_ref, lk2_ref, q_ref, k_ref, vt_ref, nw_ref, gate_hbm,
                 o_ref, s_ref, acc_ref, gate_buf, gate_sem, *, lambda_init):
    head = pl.program_id(0)
    n_q = q_ref.shape[0] // ATT_TQ
    n_steps = n_q * (n_q + 1) // 2
    assert n_steps % ATT_STEPS == 0 and ATT_STEPS % 2 == 0 and ATT_TQ == ATT_TK
    n_t = ATT_TQ // ATT_TN
    lam = (jnp.exp(jnp.sum(lq1_ref[...] * lk1_ref[...]))
           - jnp.exp(jnp.sum(lq2_ref[...] * lk2_ref[...])) + lambda_init)
    lane = lax.broadcasted_iota(jnp.int32, (ATT_TQ, HEAD_W), 1)
    neg_inf = jnp.full((1, ATT_TN), -jnp.inf, F32)

    acc_ref[...] = jnp.zeros_like(acc_ref)

    def advance(pos):
        qi, t = pos
        row_end = t == qi
        qi_n = jnp.where(row_end, qi + 1, qi)
        t_n = jnp.where(row_end, 0, t + 1)
        done = qi_n == n_q
        return jnp.where(done, n_q - 1, qi_n), jnp.where(done, n_q - 1, t_n)

    def diag_rows(n):
        return min(ATT_TK, (n + 1) * ATT_TN)

    def scores(pos, slot, masked):
        qi, t = pos
        q = q_ref[pl.ds(pl.multiple_of(qi * ATT_TQ, ATT_TQ), ATT_TQ), :]
        kb = k_ref[pl.ds(pl.multiple_of(t * ATT_TK, ATT_TK), ATT_TK), :]
        zero = jnp.zeros_like(q)
        q_maps = (jnp.where(lane < QK_DIM, q, zero), jnp.where(lane >= QK_DIM, q, zero))
        mx = []
        for c in range(2):
            for n in range(n_t):
                cols = slice(n * ATT_TN, (n + 1) * ATT_TN)
                live = diag_rows(n) if masked is True else ATT_TK
                s = lax.dot_general(kb[:live], q_maps[c][cols, :], NT,
                                    preferred_element_type=F32)
                if masked is not False:
                    kpos = lax.broadcasted_iota(jnp.int32, s.shape, 0) + t * ATT_TK
                    qpos = lax.broadcasted_iota(jnp.int32, s.shape, 1) + (qi * ATT_TQ + n * ATT_TN)
                    keep = kpos <= qpos
                    if masked is not True:
                        keep = keep | jnp.logical_not(masked)
                    s = jnp.where(keep, s, -jnp.inf)
                s_ref[slot, c, :live, cols] = s
                if live < ATT_TK:
                    s_ref[slot, c, live:, cols] = jnp.full((ATT_TK - live, ATT_TN), -jnp.inf, F32)
                mx.append(jnp.max(s, axis=0, keepdims=True))
                yield
        return tuple(mx)

    def softmax_pv(pos, slot, mx, m_run, diag=False):
        qi, t = pos
        first = t == 0
        a = qi % ACC_SLOTS
        vt = vt_ref[:, pl.ds(pl.multiple_of(t * ATT_TK, ATT_TK), ATT_TK)]
        m_out = []
        for c in range(2):
            for n in range(n_t):
                i = c * n_t + n
                cols = slice(n * ATT_TN, (n + 1) * ATT_TN)
                m_old = jnp.where(first, neg_inf, m_run[i])
                m_new = jnp.maximum(m_old, mx[i])
                alpha = jnp.exp2(m_old - m_new)
                live = diag_rows(n) if diag else ATT_TK
                p = jnp.exp2(s_ref[slot, c, :live, cols] - m_new).astype(BF16)
                upd = jnp.dot(vt[:, :live], p, preferred_element_type=F32)
                acc_ref[a, c, :, cols] = alpha * acc_ref[a, c, :, cols] + upd
                m_out.append(m_new)
                yield
        return tuple(m_out)

    def finalize(qi):
        a1 = acc_ref[qi % ACC_SLOTS, 0]
        a2 = acc_ref[qi % ACC_SLOTS, 1]
        o_t = (a1[:HEAD_W] / a1[HEAD_W:HEAD_W + 1]
               - lam * (a2[:HEAD_W] / a2[HEAD_W:HEAD_W + 1]))
        o = o_t.T
        ms = jnp.mean(o * o, axis=-1, keepdims=True)
        rows = pl.ds(pl.multiple_of(qi * ATT_TQ, ATT_TQ), ATT_TQ)
        o = o * lax.rsqrt(ms + SUBLN_EPS) * nw_ref[...] * (1.0 - lambda_init)
        o_ref[rows, :] = (o * gate_buf[qi % GATE_SLOTS]).astype(BF16)

    def on_diag(pos):
        return pos[0] == pos[1]

    def gate_dma(qi):
        slot = qi % GATE_SLOTS
        src = gate_hbm.at[pl.ds(pl.multiple_of(qi * ATT_TQ, ATT_TQ), ATT_TQ),
                          pl.ds(pl.multiple_of(head * HEAD_W, HEAD_W), HEAD_W)]
        return pltpu.make_async_copy(src, gate_buf.at[slot], gate_sem.at[slot])

    def gate_start(pos_b, live=True):
        for p in pos_b:
            @pl.when(on_diag(p) & live)
            def _():
                gate_dma(p[0]).start()

    def gate_wait(pos_b):
        for p in pos_b:
            @pl.when(on_diag(p))
            def _():
                gate_dma(p[0]).wait()

    def body_block(pos_a, pos_b, mx, m_run, masks, fin_step):
        for j in range(ATT_STEPS):
            mx, m_run = _run_interleaved(scores(pos_a[j], (j + 1) % 2, masks[j]),
                                         softmax_pv(pos_b[j], j % 2, mx, m_run, j == fin_step))
            if j == fin_step:
                finalize(pos_b[j][0])
        return mx, m_run

    def positions(pos):
        out = [pos]
        for _ in range(ATT_STEPS - 1):
            out.append(advance(out[-1]))
        return out

    def body(it, carry):
        pos_a0, pos_b0, mx, m_run = carry
        pos_a, pos_b = positions(pos_a0), positions(pos_b0)
        gate_wait(pos_b)
        gate_start(positions(advance(pos_b[-1])), it + 1 < n_steps // ATT_STEPS)
        flags = [on_diag(pos_a[j]) & (it * ATT_STEPS + 1 + j < n_steps) for j in range(ATT_STEPS)]
        count = sum(f.astype(jnp.int32) for f in flags)
        single = sum((j + 1) * f.astype(jnp.int32) for j, f in enumerate(flags))
        idx = jnp.where(count > 1, ATT_STEPS + 1, single)
        mask_sets = ([[False] * ATT_STEPS]
                     + [[j == k for j in range(ATT_STEPS)] for k in range(ATT_STEPS)]
                     + [flags])
        fin_steps = ([None] + [k + 1 if k + 1 < ATT_STEPS else None for k in range(ATT_STEPS)]
                     + [None])
        branches = [functools.partial(
            lambda ms, fs, _: body_block(pos_a, pos_b, mx, m_run, ms, fs), ms, fs)
            for ms, fs in zip(mask_sets, fin_steps)]
        mx, m_run = lax.switch(idx, branches, 0)

        in_line = jnp.where((count == 1) & (single < ATT_STEPS), single, -1)
        for j in range(ATT_STEPS):
            @pl.when(on_diag(pos_b[j]) & (in_line != j))
            def _():
                finalize(pos_b[j][0])

        return advance(pos_a[-1]), advance(pos_b[-1]), mx, m_run

    pos0 = (jnp.int32(0), jnp.int32(0))
    gate_start(positions(pos0))
    mx0 = _run_interleaved(scores(pos0, 0, True))[0]
    m0 = (neg_inf,) * (2 * n_t)
    lax.fori_loop(0, n_steps // ATT_STEPS, body, (advance(pos0), pos0, mx0, m0))


def _diff_attn(lq1, lk1, lq2, lk2, dq, dk, dvt, norm_w, dgate, lambda_init):
    T = dq.shape[0]
    lam_spec = pl.BlockSpec((1, QK_DIM), lambda h: (0, 0))
    return pl.pallas_call(
        functools.partial(_attn_kernel, lambda_init=lambda_init),
        grid=(N_HEADS,),
        in_specs=[
            lam_spec, lam_spec, lam_spec, lam_spec,
            pl.BlockSpec((T, HEAD_W), lambda h: (0, h)),
            pl.BlockSpec((T, HEAD_W), lambda h: (0, h)),
            pl.BlockSpec((None, VT_ROWS, T), lambda h: (h, 0, 0)),
            pl.BlockSpec((1, HEAD_W), lambda h: (0, 0)),
            pl.BlockSpec(memory_space=pl.ANY),
        ],
        out_specs=pl.BlockSpec((T, HEAD_W), lambda h: (0, h)),
        out_shape=jax.ShapeDtypeStruct((T, GROUP_W), BF16),
        scratch_shapes=[
            pltpu.VMEM((2, 2, ATT_TK, ATT_TQ), F32),
            pltpu.VMEM((ACC_SLOTS, 2, VT_ROWS, ATT_TQ), F32),
            pltpu.VMEM((GATE_SLOTS, ATT_TQ, HEAD_W), F32),
            pltpu.SemaphoreType.DMA((GATE_SLOTS,)),
        ],
        compiler_params=pltpu.CompilerParams(
            dimension_semantics=("arbitrary",), vmem_limit_bytes=VMEM_LIMIT),
        name="diff_attn",
    )(lq1, lk1, lq2, lk2, dq, dk, dvt, norm_w, dgate)


def _out_proj_kernel(ho_ref, do_ref, x_ref, w_ref, nw_ref, o_ref):
    y = x_ref[...]
    y = y + jnp.dot(ho_ref[...], w_ref[:GROUP_W, :], preferred_element_type=F32)
    y = y + jnp.dot(do_ref[...], w_ref[GROUP_W:, :], preferred_element_type=F32)
    ms = jnp.mean(y * y, axis=-1, keepdims=True)
    o_ref[...] = y * lax.rsqrt(ms + NORM_EPS) * nw_ref[...]


def _out_proj(ho, do, x2, w_out_bf, final_norm_w):
    T, D = x2.shape
    row = lambda i: (i, 0)
    fix = lambda i: (0, 0)
    return pl.pallas_call(
        _out_proj_kernel,
        grid=(T // PROJ_TM,),
        in_specs=[
            pl.BlockSpec((PROJ_TM, GROUP_W), row),
            pl.BlockSpec((PROJ_TM, GROUP_W), row),
            pl.BlockSpec((PROJ_TM, D), row),
            pl.BlockSpec(w_out_bf.shape, fix),
            pl.BlockSpec((1, D), fix),
        ],
        out_specs=pl.BlockSpec((PROJ_TM, D), row),
        out_shape=jax.ShapeDtypeStruct((T, D), F32),
        compiler_params=pltpu.CompilerParams(
            dimension_semantics=("arbitrary",), vmem_limit_bytes=VMEM_LIMIT),
        name="out_proj",
    )(ho, do, x2, w_out_bf, final_norm_w)


def _rope_tables(T):
    half = QK_DIM // 2
    inv_freq = 1.0 / (ROPE_THETA ** (jnp.arange(half, dtype=F32) / half))
    start = inv_freq[None, :, None] * (IN_TM * jnp.arange(T // IN_TM, dtype=F32))[:, None, None]
    offset = inv_freq[:, None] * jnp.arange(IN_TM, dtype=F32)[None, :]
    return jnp.cos(start), jnp.sin(start), jnp.cos(offset), jnp.sin(offset)


def kernel(x, norm_w, w_in, hgrn_lb_logits, hgrn_norm_w, diff_lambda_q1, diff_lambda_k1,
           diff_lambda_q2, diff_lambda_k2, diff_norm_w, w_out, final_norm_w):
    B, T, D = x.shape
    depth = norm_w.shape[0]
    assert B == 1 and depth == 1 and D == 2 * GROUP_W
    assert T % ATT_TQ == 0 and T % HGRN_TM == 0 and T % PROJ_TM == 0 and T % IN_TM == 0
    x2 = x.reshape(T, D)
    rope = _rope_tables(T)
    l = 0
    lambda_init = 0.8 - 0.6 * math.exp(-0.3 * l)
    hq, hk, hgl, hi, hgate, dq, dk, dvt, dgate = _in_proj(
        x2, norm_w[l:l + 1], w_in[l].astype(BF16), hgrn_lb_logits, rope)
    ho = _hgrn(hq, hk, hgl, hi, hgate, hgrn_norm_w[l:l + 1])
    do = _diff_attn(diff_lambda_q1[l:l + 1], diff_lambda_k1[l:l + 1],
                    diff_lambda_q2[l:l + 1], diff_lambda_k2[l:l + 1],
                    dq, dk, dvt, diff_norm_w[l:l + 1], dgate, lambda_init)
    out = _out_proj(ho, do, x2, w_out[l].astype(BF16), final_norm_w.reshape(1, D))
    return out.reshape(B, T, D)
```

```python
import functools
import math

import jax
import jax.numpy as jnp
from jax import lax
from jax.experimental import pallas as pl
from jax.experimental.pallas import tpu as pltpu

F32 = jnp.float32
BF16 = jnp.bfloat16

SUBLANES = 8
VMEM_LIMIT = 56 * 1024 * 1024

N_HEADS = 4
HEAD_W = 128
GROUP_W = N_HEADS * HEAD_W
QK_DIM = 64
ROPE_THETA = 10000.0
NORM_EPS = 1e-6
SUBLN_EPS = 1e-5

IN_TM = 1024
PROJ_TM = 2048
HGRN_CHUNK = 64
HGRN_SUB = 16
HGRN_TM = 512
HGRN_GROUP = 8
ATT_TQ = 512
ATT_TK = 512
ATT_TN = 256
ATT_STEPS = 6
GATE_WIN = 3
ACC_SLOTS = 4
BF16_ROWS = 16
VT_ROWS = HEAD_W + BF16_ROWS
LOG2E = math.log2(math.e)

NT = (((1,), (1,)), ((), ()))
TN = (((0,), (0,)), ((), ()))


def _run_interleaved(*gens):
    results = [None] * len(gens)
    live = list(range(len(gens)))
    while live:
        for i in list(live):
            try:
                next(gens[i])
            except StopIteration as stop:
                results[i] = stop.value
                live.remove(i)
    return results


def _silu(x):
    return x * (1.0 / (1.0 + jnp.exp(-x)))


def _in_proj_kernel(x_ref, nw_ref, w_ref, lbl_ref, cos_a_ref, sin_a_ref, cos_b_ref, sin_b_ref,
                    hq_ref, hk_ref, hgl_ref, hi_ref, hgate_ref,
                    dq_ref, dk_ref, dvt_ref, dgate_ref):
    x = x_ref[...]
    ms = jnp.mean(x * x, axis=-1, keepdims=True)
    h = (x * lax.rsqrt(ms + NORM_EPS) * nw_ref[...]).astype(BF16)

    def seg(i):
        return jnp.dot(h, w_ref[:, i * GROUP_W:(i + 1) * GROUP_W],
                       preferred_element_type=F32)

    lg = lbl_ref[...]
    e = jnp.exp(lg - jnp.max(lg, axis=0, keepdims=True))
    lb = e[0:1, :] / jnp.sum(e, axis=0, keepdims=True)

    hq_ref[...] = seg(0)
    f = lb + (1.0 - lb) * (1.0 / (1.0 + jnp.exp(-seg(1))))
    hk_ref[...] = 1.0 - f
    hgl_ref[...] = jnp.log2(f)
    hi_ref[...] = seg(2).astype(BF16)
    hgate_ref[...] = _silu(seg(3))

    reps = HEAD_W // (QK_DIM // 2)
    cos_a, sin_a = cos_a_ref[...], sin_a_ref[...]
    cos_b, sin_b = cos_b_ref[...], sin_b_ref[...]
    cos_half = cos_a * cos_b - sin_a * sin_b
    sin_half = sin_a * cos_b + cos_a * sin_b
    cos = jnp.concatenate([cos_half] * reps, axis=0).T
    sin = jnp.concatenate([-sin_half, sin_half] * (reps // 2), axis=0).T
    lane = lax.broadcasted_iota(jnp.int32, cos.shape, 1)
    first_half = (lane % QK_DIM) < (QK_DIM // 2)

    def rope(t, scale):
        outs = []
        for hh in range(N_HEADS):
            th = t[:, hh * HEAD_W:(hh + 1) * HEAD_W]
            up = pltpu.roll(th, HEAD_W - QK_DIM // 2, 1)
            dn = pltpu.roll(th, QK_DIM // 2, 1)
            partner = jnp.where(first_half, up, dn)
            outs.append(((th * cos + partner * sin) * scale).astype(BF16))
        return jnp.concatenate(outs, axis=1)

    dq_ref[...] = rope(seg(4), QK_DIM ** -0.5 * LOG2E)
    dk_ref[...] = rope(seg(5), 1.0)
    vt = seg(6).T.astype(BF16)
    for hh in range(N_HEADS):
        dvt_ref[hh, :HEAD_W, :] = vt[hh * HEAD_W:(hh + 1) * HEAD_W, :]
        dvt_ref[hh, HEAD_W:, :] = jnp.ones((VT_ROWS - HEAD_W, IN_TM), BF16)
    dgate_ref[...] = _silu(seg(7))


def _in_proj(x2, norm_w, w_in_bf, lb_logits, rope):
    T, D = x2.shape
    n = T // IN_TM
    row = lambda i: (i, 0)
    fix = lambda i: (0, 0)
    f32_out = jax.ShapeDtypeStruct((T, GROUP_W), F32)
    bf_out = jax.ShapeDtypeStruct((T, GROUP_W), BF16)
    blk = pl.BlockSpec((IN_TM, GROUP_W), row)
    return pl.pallas_call(
        _in_proj_kernel,
        grid=(n,),
        in_specs=[
            pl.BlockSpec((IN_TM, D), row),
            pl.BlockSpec((1, D), fix),
            pl.BlockSpec(w_in_bf.shape, fix, pipeline_mode=pl.Buffered(1)),
            pl.BlockSpec(lb_logits.shape, fix),
            pl.BlockSpec((None, QK_DIM // 2, 1), lambda i: (i, 0, 0)),
            pl.BlockSpec((None, QK_DIM // 2, 1), lambda i: (i, 0, 0)),
            pl.BlockSpec((QK_DIM // 2, IN_TM), fix),
            pl.BlockSpec((QK_DIM // 2, IN_TM), fix),
        ],
        out_specs=[blk, blk, blk, blk, blk, blk, blk,
                   pl.BlockSpec((N_HEADS, VT_ROWS, IN_TM), lambda i: (0, 0, i)), blk],
        out_shape=[f32_out, f32_out, f32_out, bf_out, f32_out, bf_out, bf_out,
                   jax.ShapeDtypeStruct((N_HEADS, VT_ROWS, T), BF16), f32_out],
        compiler_params=pltpu.CompilerParams(
            dimension_semantics=("arbitrary",), vmem_limit_bytes=VMEM_LIMIT),
        name="in_proj",
    )(x2, norm_w, w_in_bf, lb_logits, *rope)


def _hgrn_kernel(q_ref, k_ref, g_ref, v_ref, gate_ref, nw_ref, o_ref, st_ref, c_ref):
    C, SB = HGRN_CHUNK, HGRN_SUB

    @pl.when(pl.program_id(0) == 0)
    def _():
        st_ref[...] = jnp.zeros_like(st_ref)

    ti = lax.broadcasted_iota(jnp.int32, (C, C), 0)
    si = lax.broadcasted_iota(jnp.int32, (C, C), 1)
    tri = (si <= ti).astype(F32)
    a_row = lax.broadcasted_iota(jnp.int32, (SUBLANES, C), 0)
    a_lane = lax.broadcasted_iota(jnp.int32, (SUBLANES, C), 1)
    nw = nw_ref[...]

    def chunk_heads(c, slot):
        r0 = pl.multiple_of(c * C, C)
        rows = pl.ds(r0, C)
        b_all = jnp.dot(tri, g_ref[rows, :], preferred_element_type=F32,
                        precision=lax.Precision.HIGHEST)
        c_ref[slot] = b_all - jnp.log2(jnp.maximum(k_ref[rows, :], 0.0))

        def head(hh):
            cols = slice(hh * HEAD_W, (hh + 1) * HEAD_W)
            b = b_all[:, cols]
            q = q_ref[rows, cols]
            k = k_ref[rows, cols]
            v_bf = v_ref[rows, cols]
            st = st_ref[hh]
            inter = lax.dot_general((q * jnp.exp2(b)).astype(BF16), st.astype(BF16),
                                    NT, preferred_element_type=F32)
            b_last = b[C - 1:C, :]
            k_dec = (k * jnp.exp2(b_last - b)).astype(BF16)
            st_ref[hh] = st * jnp.exp2(b_last) + lax.dot_general(
                v_bf, k_dec, TN, preferred_element_type=F32)

            q_near, k_near = [], []
            for blk in range(C // SB):
                lo, mid = blk * SB, blk * SB + SUBLANES
                ref = b[mid - 1:mid, :]
                q_near.append(q[mid:mid + SUBLANES, :] * jnp.exp2(b[mid:mid + SUBLANES, :] - ref))
                k_near.append(k[lo:mid, :] * jnp.exp2(ref - b[lo:mid, :]))
                k_near.append(jnp.zeros((SUBLANES, HEAD_W), F32))
            a_near = lax.dot_general(jnp.concatenate(q_near, axis=0).astype(BF16),
                                     jnp.concatenate(k_near, axis=0).astype(BF16),
                                     NT, preferred_element_type=F32)
            yield
            pieces = []
            for blk in range(C // SB):
                lo = blk * SB
                near = a_near[blk * SUBLANES:(blk + 1) * SUBLANES, :]
                if blk > 0:
                    ref = b[lo - 1:lo, :]
                    q_t = (q[lo:lo + SB, :] * jnp.exp2(b[lo:lo + SB, :] - ref)).astype(BF16)
                    k_t = (k[:lo, :] * jnp.exp2(ref - b[:lo, :])).astype(BF16)
                    k_t = jnp.concatenate([k_t, jnp.zeros((C - lo, HEAD_W), BF16)], axis=0)
                    a_far = lax.dot_general(q_t, k_t, NT, preferred_element_type=F32)
                    halves = [a_far[:SUBLANES, :],
                              jnp.where(a_lane >= lo, near, a_far[SUBLANES:, :])]
                else:
                    halves = [jnp.zeros((SUBLANES, C), F32), near]
                for s in range(SB):
                    half = s // SUBLANES
                    r = lo + half * SUBLANES
                    c_s = c_ref[slot, pl.ds(lo + s, 1), cols]
                    w = q[r:r + SUBLANES, :] * jnp.exp2(b[r:r + SUBLANES, :] - c_s)
                    col = jnp.sum(w, axis=-1, keepdims=True)
                    halves[half] = jnp.where(a_lane == lo + s, col, halves[half])
                for half in range(2):
                    keep = a_lane <= a_row + (lo + half * SUBLANES)
                    pieces.append(jnp.where(keep, halves[half], 0.0))
                yield
            a = jnp.concatenate(pieces, axis=0).astype(BF16)
            o = inter + jnp.dot(a, v_bf, preferred_element_type=F32)
            ms = jnp.mean(o * o, axis=-1, keepdims=True)
            o_ref[rows, cols] = (o * lax.rsqrt(ms + NORM_EPS) * nw
                                 * gate_ref[rows, cols]).astype(BF16)

        return [head(hh) for hh in range(N_HEADS)]

    def chunk_group(cg, carry):
        gens = []
        for j in range(HGRN_GROUP):
            gens += chunk_heads(HGRN_GROUP * cg + j, j)
        _run_interleaved(*gens)
        return carry

    n_groups = HGRN_TM // (HGRN_GROUP * C)
    lax.fori_loop(0, n_groups, chunk_group, 0, unroll=n_groups)


def _hgrn(hq, hk, hgl, hi, hgate, norm_w):
    T = hq.shape[0]
    blk = pl.BlockSpec((HGRN_TM, GROUP_W), lambda i: (i, 0))
    return pl.pallas_call(
        _hgrn_kernel,
        grid=(T // HGRN_TM,),
        in_specs=[blk, blk, blk, blk, blk, pl.BlockSpec((1, HEAD_W), lambda i: (0, 0))],
        out_specs=blk,
        out_shape=jax.ShapeDtypeStruct((T, GROUP_W), BF16),
        scratch_shapes=[pltpu.VMEM((N_HEADS, HEAD_W, HEAD_W), F32),
                        pltpu.VMEM((HGRN_GROUP, HGRN_CHUNK, GROUP_W), F32)],
        compiler_params=pltpu.CompilerParams(
            dimension_semantics=("arbitrary",), vmem_limit_bytes=VMEM_LIMIT),
        name="hgrn2",
    )(hq, hk, hgl, hi, hgate, norm_w)


def _attn_kernel(lq1_ref, lk1_ref, lq2_ref, lk2_ref, q_ref, k_ref, vt_ref, nw_ref, gate_hbm,
                 o_ref, s_ref, acc_ref, gate_buf, gate_sem, *, lambda_init):
    head = pl.program_id(0)
    n_q = q_ref.shape[0] // ATT_TQ
    assert n_q >= GATE_WIN and sum(range(1, GATE_WIN + 1)) >= ATT_STEPS
    n_steps = n_q * (n_q + 1) // 2
    assert n_steps % ATT_STEPS == 0 and ATT_STEPS % 2 == 0 and ATT_TQ == ATT_TK
    n_t = ATT_TQ // ATT_TN
    lam = (jnp.exp(jnp.sum(lq1_ref[...] * lk1_ref[...]))
           - jnp.exp(jnp.sum(lq2_ref[...] * lk2_ref[...])) + lambda_init)
    lane = lax.broadcasted_iota(jnp.int32, (ATT_TQ, HEAD_W), 1)
    neg_inf = jnp.full((1, ATT_TN), -jnp.inf, F32)

    acc_ref[...] = jnp.zeros_like(acc_ref)

    def advance(pos):
        qi, t = pos
        row_end = t == qi
        qi_n = jnp.where(row_end, qi + 1, qi)
        t_n = jnp.where(row_end, 0, t + 1)
        done = qi_n == n_q
        return jnp.where(done, n_q - 1, qi_n), jnp.where(done, n_q - 1, t_n)

    def diag_rows(n):
        return min(ATT_TK, (n + 1) * ATT_TN)

    def scores(pos, slot, masked):
        qi, t = pos
        q = q_ref[pl.ds(pl.multiple_of(qi * ATT_TQ, ATT_TQ), ATT_TQ), :]
        kb = k_ref[pl.ds(pl.multiple_of(t * ATT_TK, ATT_TK), ATT_TK), :]
        zero = jnp.zeros_like(q)
        q_maps = (jnp.where(lane < QK_DIM, q, zero), jnp.where(lane >= QK_DIM, q, zero))
        mx = []
        for c in range(2):
            for n in range(n_t):
                cols = slice(n * ATT_TN, (n + 1) * ATT_TN)
                live = diag_rows(n) if masked is True else ATT_TK
                s = lax.dot_general(kb[:live], q_maps[c][cols, :], NT,
                                    preferred_element_type=F32)
                if masked is not False:
                    kpos = lax.broadcasted_iota(jnp.int32, s.shape, 0) + t * ATT_TK
                    qpos = lax.broadcasted_iota(jnp.int32, s.shape, 1) + (qi * ATT_TQ + n * ATT_TN)
                    keep = kpos <= qpos
                    if masked is not True:
                        keep = keep | jnp.logical_not(masked)
                    s = jnp.where(keep, s, -jnp.inf)
                s_ref[slot, c, :live, cols] = s
                if live < ATT_TK:
                    s_ref[slot, c, live:, cols] = jnp.full((ATT_TK - live, ATT_TN), -jnp.inf, F32)
                mx.append(jnp.max(s, axis=0, keepdims=True))
                yield
        return tuple(mx)

    def softmax_pv(pos, slot, mx, m_run, diag=False):
        qi, t = pos
        first = t == 0
        a = qi % ACC_SLOTS
        vt = vt_ref[:, pl.ds(pl.multiple_of(t * ATT_TK, ATT_TK), ATT_TK)]
        m_out = []
        for c in range(2):
            for n in range(n_t):
                i = c * n_t + n
                cols = slice(n * ATT_TN, (n + 1) * ATT_TN)
                m_old = jnp.where(first, neg_inf, m_run[i])
                m_new = jnp.maximum(m_old, mx[i])
                alpha = jnp.exp2(m_old - m_new)
                live = diag_rows(n) if diag else ATT_TK
                p = jnp.exp2(s_ref[slot, c, :live, cols] - m_new).astype(BF16)
                upd = jnp.dot(vt[:, :live], p, preferred_element_type=F32)
                acc_ref[a, c, :, cols] = alpha * acc_ref[a, c, :, cols] + upd
                m_out.append(m_new)
                yield
        return tuple(m_out)

    def finalize(qi, gate_ix):
        g_slot, g_q0 = gate_ix
        g_rows = pl.ds(pl.multiple_of((qi - g_q0) * ATT_TQ, ATT_TQ), ATT_TQ)
        a1 = acc_ref[qi % ACC_SLOTS, 0]
        a2 = acc_ref[qi % ACC_SLOTS, 1]
        o_t = (a1[:HEAD_W] / a1[HEAD_W:HEAD_W + 1]
               - lam * (a2[:HEAD_W] / a2[HEAD_W:HEAD_W + 1]))
        o = o_t.T
        ms = jnp.mean(o * o, axis=-1, keepdims=True)
        rows = pl.ds(pl.multiple_of(qi * ATT_TQ, ATT_TQ), ATT_TQ)
        o = o * lax.rsqrt(ms + SUBLN_EPS) * nw_ref[...] * (1.0 - lambda_init)
        o_ref[rows, :] = (o * gate_buf[g_slot, g_rows, :]).astype(BF16)

    def on_diag(pos):
        return pos[0] == pos[1]

    def gate_window(pos_first):
        return jnp.minimum(pos_first[0], n_q - GATE_WIN)

    def gate_dma(q0, slot):
        src = gate_hbm.at[pl.ds(pl.multiple_of(q0 * ATT_TQ, ATT_TQ), GATE_WIN * ATT_TQ),
                          pl.ds(pl.multiple_of(head * HEAD_W, HEAD_W), HEAD_W)]
        return pltpu.make_async_copy(src, gate_buf.at[slot], gate_sem.at[slot])

    def body_block(pos_a, pos_b, mx, m_run, gate_ix, masks, fin_step):
        for j in range(ATT_STEPS):
            mx, m_run = _run_interleaved(scores(pos_a[j], (j + 1) % 2, masks[j]),
                                         softmax_pv(pos_b[j], j % 2, mx, m_run, j == fin_step))
            if j == fin_step:
                finalize(pos_b[j][0], gate_ix)
        return mx, m_run

    def positions(pos):
        out = [pos]
        for _ in range(ATT_STEPS - 1):
            out.append(advance(out[-1]))
        return out

    def body(it, carry):
        pos_a0, pos_b0, mx, m_run = carry
        pos_a, pos_b = positions(pos_a0), positions(pos_b0)
        gate_ix = (it % 2, gate_window(pos_b[0]))
        gate_dma(gate_ix[1], gate_ix[0]).wait()
        gate_dma(gate_window(advance(pos_b[-1])), 1 - gate_ix[0]).start()
        flags = [on_diag(pos_a[j]) & (it * ATT_STEPS + 1 + j < n_steps) for j in range(ATT_STEPS)]
        count = sum(f.astype(jnp.int32) for f in flags)
        single = sum((j + 1) * f.astype(jnp.int32) for j, f in enumerate(flags))
        idx = jnp.where(count > 1, ATT_STEPS + 1, single)
        mask_sets = ([[False] * ATT_STEPS]
                     + [[j == k for j in range(ATT_STEPS)] for k in range(ATT_STEPS)]
                     + [flags])
        fin_steps = ([None] + [k + 1 if k + 1 < ATT_STEPS else None for k in range(ATT_STEPS)]
                     + [None])
        branches = [functools.partial(
            lambda ms, fs, _: body_block(pos_a, pos_b, mx, m_run, gate_ix, ms, fs), ms, fs)
            for ms, fs in zip(mask_sets, fin_steps)]
        mx, m_run = lax.switch(idx, branches, 0)

        in_line = jnp.where((count == 1) & (single < ATT_STEPS), single, -1)
        for j in range(ATT_STEPS):
            @pl.when(on_diag(pos_b[j]) & (in_line != j))
            def _():
                finalize(pos_b[j][0], gate_ix)

        return advance(pos_a[-1]), advance(pos_b[-1]), mx, m_run

    pos0 = (jnp.int32(0), jnp.int32(0))
    gate_dma(gate_window(pos0), 0).start()
    mx0 = _run_interleaved(scores(pos0, 0, True))[0]
    m0 = (neg_inf,) * (2 * n_t)
    n_bodies = n_steps // ATT_STEPS
    lax.fori_loop(0, n_bodies, body, (advance(pos0), pos0, mx0, m0))
    gate_dma(0, n_bodies % 2).wait()


def _diff_attn(lq1, lk1, lq2, lk2, dq, dk, dvt, norm_w, dgate, lambda_init):
    T = dq.shape[0]
    lam_spec = pl.BlockSpec((1, QK_DIM), lambda h: (0, 0))
    return pl.pallas_call(
        functools.partial(_attn_kernel, lambda_init=lambda_init),
        grid=(N_HEADS,),
        in_specs=[
            lam_spec, lam_spec, lam_spec, lam_spec,
            pl.BlockSpec((T, HEAD_W), lambda h: (0, h)),
            pl.BlockSpec((T, HEAD_W), lambda h: (0, h)),
            pl.BlockSpec((None, VT_ROWS, T), lambda h: (h, 0, 0)),
            pl.BlockSpec((1, HEAD_W), lambda h: (0, 0)),
            pl.BlockSpec(memory_space=pl.ANY),
        ],
        out_specs=pl.BlockSpec((T, HEAD_W), lambda h: (0, h)),
        out_shape=jax.ShapeDtypeStruct((T, GROUP_W), BF16),
        scratch_shapes=[
            pltpu.VMEM((2, 2, ATT_TK, ATT_TQ), F32),
            pltpu.VMEM((ACC_SLOTS, 2, VT_ROWS, ATT_TQ), F32),
            pltpu.VMEM((2, GATE_WIN * ATT_TQ, HEAD_W), F32),
            pltpu.SemaphoreType.DMA((2,)),
        ],
        compiler_params=pltpu.CompilerParams(
            dimension_semantics=("arbitrary",), vmem_limit_bytes=VMEM_LIMIT),
        name="diff_attn",
    )(lq1, lk1, lq2, lk2, dq, dk, dvt, norm_w, dgate)


def _out_proj_kernel(ho_ref, do_ref, x_ref, w_ref, nw_ref, o_ref):
    y = x_ref[...]
    y = y + jnp.dot(ho_ref[...], w_ref[:GROUP_W, :], preferred_element_type=F32)
    y = y + jnp.dot(do_ref[...], w_ref[GROUP_W:, :], preferred_element_type=F32)
    ms = jnp.mean(y * y, axis=-1, keepdims=True)
    o_ref[...] = y * lax.rsqrt(ms + NORM_EPS) * nw_ref[...]


def _out_proj(ho, do, x2, w_out_bf, final_norm_w):
    T, D = x2.shape
    row = lambda i: (i, 0)
    fix = lambda i: (0, 0)
    return pl.pallas_call(
        _out_proj_kernel,
        grid=(T // PROJ_TM,),
        in_specs=[
            pl.BlockSpec((PROJ_TM, GROUP_W), row),
            pl.BlockSpec((PROJ_TM, GROUP_W), row),
            pl.BlockSpec((PROJ_TM, D), row),
            pl.BlockSpec(w_out_bf.shape, fix),
            pl.BlockSpec((1, D), fix),
        ],
        out_specs=pl.BlockSpec((PROJ_TM, D), row),
        out_shape=jax.ShapeDtypeStruct((T, D), F32),
        compiler_params=pltpu.CompilerParams(
            dimension_semantics=("arbitrary",), vmem_limit_bytes=VMEM_LIMIT),
        name="out_proj",
    )(ho, do, x2, w_out_bf, final_norm_w)


def _rope_tables(T):
    half = QK_DIM // 2
    inv_freq = 1.0 / (ROPE_THETA ** (jnp.arange(half, dtype=F32) / half))
    start = inv_freq[None, :, None] * (IN_TM * jnp.arange(T // IN_TM, dtype=F32))[:, None, None]
    offset = inv_freq[:, None] * jnp.arange(IN_TM, dtype=F32)[None, :]
    return jnp.cos(start), jnp.sin(start), jnp.cos(offset), jnp.sin(offset)


def kernel(x, norm_w, w_in, hgrn_lb_logits, hgrn_norm_w, diff_lambda_q1, diff_lambda_k1,
           diff_lambda_q2, diff_lambda_k2, diff_norm_w, w_out, final_norm_w):
    B, T, D = x.shape
    depth = norm_w.shape[0]
    assert B == 1 and depth == 1 and D == 2 * GROUP_W
    assert T % ATT_TQ == 0 and T % HGRN_TM == 0 and T % PROJ_TM == 0 and T % IN_TM == 0
    x2 = x.reshape(T, D)
    rope = _rope_tables(T)
    l = 0
    lambda_init = 0.8 - 0.6 * math.exp(-0.3 * l)
    hq, hk, hgl, hi, hgate, dq, dk, dvt, dgate = _in_proj(
        x2, norm_w[l:l + 1], w_in[l].astype(BF16), hgrn_lb_logits, rope)
    ho = _hgrn(hq, hk, hgl, hi, hgate, hgrn_norm_w[l:l + 1])
    do = _diff_attn(diff_lambda_q1[l:l + 1], diff_lambda_k1[l:l + 1],
                    diff_lambda_q2[l:l + 1], diff_lambda_k2[l:l + 1],
                    dq, dk, dvt, diff_norm_w[l:l + 1], dgate, lambda_init)
    out = _out_proj(ho, do, x2, w_out[l].astype(BF16), final_norm_w.reshape(1, D))
    return out.reshape(B, T, D)
```
